```python
import jax, jax.numpy as jnp
from jax import lax
import numpy as np

D_MODEL = 1024
BATCH = 8
SEQ = 2048
DEPTH = 2
DEC_BATCH = 128
DEC_SEQ = 8
PAST_LEN = 2048
PAGE_SIZE = 128

HEAD_DIM = 64
RWKV_HEADS = 4
RWKV_W = RWKV_HEADS * HEAD_DIM
W_LORA = 32
A_LORA = 32
G_LORA = 64
RWKV_PROJ = 3 * RWKV_W + W_LORA + A_LORA + G_LORA
GN_EPS = 64e-5
GMLP_GROUPS = 4
GMLP_W = GMLP_GROUPS * HEAD_DIM
CHUNK = 128
LN_EPS = 1e-5
SB_HEADS = 8
SB_W = SB_HEADS * HEAD_DIM
Q_BLOCK = 128
SB_BIAS_INIT = -6.0
D_MIX = RWKV_W + GMLP_W + SB_W
P_TOTAL = RWKV_PROJ + 2 * GMLP_W + 3 * SB_W
N_GROUPS = 4
EXPERTS_PER_GROUP = 4
N_EXPERTS = N_GROUPS * EXPERTS_PER_GROUP
TOP_K = 2
D_EXPERT = 256
N_MOD = 6
RMS_EPS = 1e-6

kernel_name = "hymba_rwkv7_gmlp_stickbreak_hmoe_step"


def rms_norm(x, g):
    x32 = x.astype(jnp.float32)
    y = x32 * lax.rsqrt(jnp.mean(x32 * x32, axis=-1, keepdims=True) + RMS_EPS)
    return (y * g.astype(jnp.float32)).astype(x.dtype)


def rwkv_step(S, inp):
    r, dec, k, v, a_vec, b_vec = inp
    Sa = jnp.einsum('bhij,bhj->bhi', S, a_vec)
    S = S * dec[:, :, None, :] + Sa[..., None] * b_vec[:, :, None, :] + v[..., None] * k[:, :, None, :]
    y = jnp.einsum('bhij,bhj->bhi', S, r)
    return S, y


def rwkv_mix(p, p_prev, S0, mu, w0, w_up, a0, a_up, g_up, k_k, k_a, r_k, ln_g, ln_b):
    B, T, _ = p.shape
    f32 = jnp.float32
    prev = jnp.concatenate([p_prev[:, None, :].astype(p.dtype), p[:, :-1]], axis=1)
    ps = p + mu * (prev - p)
    o = 3 * RWKV_W
    r = ps[..., 0:RWKV_W]
    k = ps[..., RWKV_W:2 * RWKV_W]
    v = ps[..., 2 * RWKV_W:o]
    wd = ps[..., o:o + W_LORA]
    ad = ps[..., o + W_LORA:o + W_LORA + A_LORA]
    gd = ps[..., o + W_LORA + A_LORA:]
    w = -jax.nn.softplus(-(w0 + jnp.tanh(wd) @ w_up).astype(f32)) - 0.5
    decay = jnp.exp(-jnp.exp(w))
    a = jax.nn.sigmoid((a0 + ad @ a_up).astype(f32))
    g = jax.nn.sigmoid(gd) @ g_up

    def heads(t):
        return t.astype(f32).reshape(B, T, RWKV_HEADS, HEAD_DIM)

    kk = heads(k * k_k)
    kk = kk * lax.rsqrt(jnp.maximum(jnp.sum(kk * kk, axis=-1, keepdims=True), 1e-24))
    ah = heads(a)
    kh = heads(k) * (1.0 + (ah - 1.0) * k_a.reshape(RWKV_HEADS, HEAD_DIM).astype(f32))
    rh, vh, dh = heads(r), heads(v), heads(decay)
    a_vec = -kk
    b_vec = kk * ah
    xs = tuple(jnp.swapaxes(t, 0, 1) for t in (rh, dh, kh, vh, a_vec, b_vec))
    S, y = lax.scan(rwkv_step, S0.astype(f32), xs)
    y = jnp.swapaxes(y, 0, 1)
    mean = jnp.mean(y, axis=-1, keepdims=True)
    var = jnp.mean(jnp.square(y - mean), axis=-1, keepdims=True)
    y = (y - mean) * lax.rsqrt(var + GN_EPS)
    y = y * ln_g.reshape(RWKV_HEADS, HEAD_DIM).astype(f32) + ln_b.reshape(RWKV_HEADS, HEAD_DIM).astype(f32)
    bonus = jnp.sum(rh * kh * r_k.astype(f32), axis=-1, keepdims=True) * vh
    out = ((y + bonus).reshape(B, T, RWKV_W) * g.astype(f32)).astype(p.dtype)
    return out, S.astype(S0.dtype), p[:, -1]


def gmlp_mix(u, v, ln_g, ln_b, ws, bs):
    B, T, _ = u.shape
    f32 = jnp.float32
    u = jax.nn.gelu(u)
    v = jax.nn.gelu(v).astype(f32).reshape(B, T, GMLP_GROUPS, HEAD_DIM)
    mean = jnp.mean(v, axis=-1, keepdims=True)
    var = jnp.mean(jnp.square(v - mean), axis=-1, keepdims=True)
    v = (v - mean) * lax.rsqrt(var + LN_EPS)
    v = v * ln_g.reshape(GMLP_GROUPS, HEAD_DIM).astype(f32) + ln_b.reshape(GMLP_GROUPS, HEAD_DIM).astype(f32)
    n_chunks = -(-T // CHUNK)
    pad = n_chunks * CHUNK - T
    vp = jnp.pad(v, ((0, 0), (0, pad), (0, 0), (0, 0))).reshape(B, n_chunks, CHUNK, GMLP_GROUPS, HEAD_DIM)
    wm = ws.astype(f32) * jnp.tril(jnp.ones((CHUNK, CHUNK), f32))
    mixed = jnp.einsum('gts,bnsgc->bntgc', wm, vp) + jnp.swapaxes(bs.astype(f32), 0, 1)[None, None, :, :, None]
    mixed = mixed.reshape(B, n_chunks * CHUNK, GMLP_W)[:, :T]
    y = (u.astype(f32) * mixed).astype(u.dtype)
    return y, v.reshape(B, T, GMLP_W).astype(u.dtype)


def sb_block(q, k, v, q_pos, k_pos, bias):
    f32 = jnp.float32
    z = jnp.einsum('bqhd,bkhd->bhqk', q.astype(f32), k.astype(f32)) * (HEAD_DIM ** -0.5)
    z = z + bias.astype(f32)[None, :, None, None]
    mask = k_pos[None, :] < q_pos[:, None]
    log_not = jnp.where(mask, jax.nn.log_sigmoid(-z), 0.0)
    shifted = jnp.concatenate([log_not[..., 1:], jnp.zeros_like(log_not[..., :1])], axis=-1)
    tail = lax.cumsum(shifted, axis=3, reverse=True)
    A = jnp.where(mask, jnp.exp(jax.nn.log_sigmoid(z) + tail), 0.0)
    return jnp.einsum('bhqk,bkhd->bqhd', A, v.astype(f32)).astype(q.dtype)


def sb_sweep(q, k_all, v_all, q_offset, bias):
    Tq = q.shape[1]
    outs = []
    for s0 in range(0, Tq, Q_BLOCK):
        s1 = min(s0 + Q_BLOCK, Tq)
        kend = q_offset + s1
        q_pos = q_offset + jnp.arange(s0, s1)
        k_pos = jnp.arange(kend)
        outs.append(sb_block(q[:, s0:s1], k_all[:, :kend], v_all[:, :kend], q_pos, k_pos, bias))
    return jnp.concatenate(outs, axis=1)


def hier_moe(h, rg_w, rg_b, re_w, re_b, wg, wu, wd):
    f32 = jnp.float32
    N = h.shape[0]
    pg = jax.nn.softmax((h @ rg_w + rg_b).astype(f32), axis=-1)
    pg_top, g_idx = lax.top_k(pg, 1)
    el = (h @ re_w + re_b).astype(f32).reshape(N, N_GROUPS, EXPERTS_PER_GROUP)
    el = jnp.take_along_axis(el, g_idx[:, :, None], axis=1)[:, 0]
    pe = jax.nn.softmax(el, axis=-1)
    pe_top, e_idx = lax.top_k(pe, TOP_K)
    pe_top = pe_top / jnp.sum(pe_top, axis=-1, keepdims=True)
    gate = pg_top * pe_top
    ids = g_idx * EXPERTS_PER_GROUP + e_idx
    dense_w = jnp.sum(jax.nn.one_hot(ids, N_EXPERTS, dtype=f32) * gate[..., None], axis=1)
    hid = jax.nn.silu(jnp.einsum('nd,edf->nef', h, wg)) * jnp.einsum('nd,edf->nef', h, wu)
    hid = hid * dense_w[:, :, None].astype(h.dtype)
    return jnp.einsum('nef,efd->nd', hid, wd)


def trunk_layer(x, c, lw, rwkv_S0, rwkv_shift0, past_k, past_v):
    B, T, _ = x.shape
    mod = (jax.nn.silu(c) @ lw['w_mod'] + lw['b_mod']).reshape(B, N_MOD, 1, D_MODEL)
    shift1, scale1, gate1, shift2, scale2, gate2 = (mod[:, i] for i in range(N_MOD))
    h = rms_norm(x, lw['norm1_g']) * (1.0 + scale1) + shift1
    p = h @ lw['w_in']
    o = RWKV_PROJ
    p_rwkv = p[..., :o]
    g_u = p[..., o:o + GMLP_W]
    g_v = p[..., o + GMLP_W:o + 2 * GMLP_W]
    o = o + 2 * GMLP_W
    q = p[..., o:o + SB_W].reshape(B, T, SB_HEADS, HEAD_DIM)
    k = p[..., o + SB_W:o + 2 * SB_W].reshape(B, T, SB_HEADS, HEAD_DIM)
    v = p[..., o + 2 * SB_W:o + 3 * SB_W].reshape(B, T, SB_HEADS, HEAD_DIM)

    y_rwkv, S_new, shift_new = rwkv_mix(p_rwkv, rwkv_shift0, rwkv_S0, lw['rwkv_mu'], lw['rwkv_w0'], lw['rwkv_w_up'],
                                        lw['rwkv_a0'], lw['rwkv_a_up'], lw['rwkv_g_up'], lw['rwkv_k_k'],
                                        lw['rwkv_k_a'], lw['rwkv_r_k'], lw['rwkv_ln_g'], lw['rwkv_ln_b'])
    y_gmlp, v_rows = gmlp_mix(g_u, g_v, lw['gmlp_ln_g'], lw['gmlp_ln_b'], lw['gmlp_ws'], lw['gmlp_bs'])
    if past_k is None:
        k_all, v_all, q_off = k, v, 0
    else:
        k_all = jnp.concatenate([past_k.astype(k.dtype), k], axis=1)
        v_all = jnp.concatenate([past_v.astype(v.dtype), v], axis=1)
        q_off = past_k.shape[1]
    y_sb = sb_sweep(q, k_all, v_all, q_off, lw['sb_bias']).reshape(B, T, SB_W)

    mixed = jnp.concatenate([y_rwkv, y_gmlp, y_sb], axis=-1) @ lw['w_out']
    x = x + gate1 * mixed
    h2 = rms_norm(x, lw['norm2_g']) * (1.0 + scale2) + shift2
    moe = hier_moe(h2.reshape(B * T, D_MODEL), lw['router_g_w'], lw['router_g_b'], lw['router_e_w'],
                   lw['router_e_b'], lw['exp_wg'], lw['exp_wu'], lw['exp_wd']).reshape(B, T, D_MODEL)
    x = x + gate2 * moe
    return x, S_new, shift_new, k, v, v_rows


def setup_inputs(seed: int = 0) -> dict:
    key = jax.random.key(seed)
    ks = iter(jax.random.split(key, 64))
    f32 = jnp.float32

    def nrm(shape, scale=1.0):
        return scale * jax.random.normal(next(ks), shape, f32)

    n_pages = PAST_LEN // PAGE_SIZE
    n_pool = (DEC_BATCH * n_pages * 5) // 4
    x_prompt = nrm((BATCH, SEQ, D_MODEL))
    x_sample = nrm((DEC_BATCH, DEC_SEQ, D_MODEL))
    cache_k = nrm((DEPTH, n_pool, PAGE_SIZE, SB_HEADS, HEAD_DIM))
    cache_v = nrm((DEPTH, n_pool, PAGE_SIZE, SB_HEADS, HEAD_DIM))
    page_table = jax.random.permutation(next(ks), n_pool)[:DEC_BATCH * n_pages].reshape(DEC_BATCH, n_pages).astype(jnp.int32)
    state_rwkv = nrm((DEPTH, DEC_BATCH, RWKV_HEADS, HEAD_DIM, HEAD_DIM), 0.1)
    state_rwkv_shift = nrm((DEPTH, DEC_BATCH, RWKV_PROJ))
    c_prompt = nrm((BATCH, D_MODEL))
    c_sample = nrm((DEC_BATCH, D_MODEL))
    b_mod = nrm((DEPTH, N_MOD, D_MODEL), 0.02).at[:, 2].add(1.0).at[:, 5].add(1.0).reshape(DEPTH, N_MOD * D_MODEL)
    return {
        'x_prompt': x_prompt,
        'x_sample': x_sample,
        'cache_k': cache_k,
        'cache_v': cache_v,
        'page_table': page_table,
        'state_rwkv': state_rwkv,
        'state_rwkv_shift': state_rwkv_shift,
        'c_prompt': c_prompt,
        'c_sample': c_sample,
        'norm1_g': 1.0 + nrm((DEPTH, D_MODEL), 0.02),
        'norm2_g': 1.0 + nrm((DEPTH, D_MODEL), 0.02),
        'w_mod': nrm((DEPTH, D_MODEL, N_MOD * D_MODEL), 0.1 * D_MODEL ** -0.5),
        'b_mod': b_mod,
        'w_in': nrm((DEPTH, D_MODEL, P_TOTAL), D_MODEL ** -0.5),
        'rwkv_mu': jax.random.uniform(next(ks), (DEPTH, RWKV_PROJ), f32),
        'rwkv_w0': jax.random.uniform(next(ks), (DEPTH, RWKV_W), f32, -6.5, -1.5),
        'rwkv_w_up': nrm((DEPTH, W_LORA, RWKV_W), 0.1),
        'rwkv_a0': nrm((DEPTH, RWKV_W), 0.1),
        'rwkv_a_up': nrm((DEPTH, A_LORA, RWKV_W), 0.1),
        'rwkv_g_up': nrm((DEPTH, G_LORA, RWKV_W), G_LORA ** -0.5),
        'rwkv_k_k': 0.85 + nrm((DEPTH, RWKV_W), 0.02),
        'rwkv_k_a': 1.0 + nrm((DEPTH, RWKV_W), 0.02),
        'rwkv_r_k': nrm((DEPTH, RWKV_HEADS, HEAD_DIM), 0.1),
        'rwkv_ln_g': 1.0 + nrm((DEPTH, RWKV_W), 0.02),
        'rwkv_ln_b': nrm((DEPTH, RWKV_W), 0.01),
        'gmlp_ln_g': 1.0 + nrm((DEPTH, GMLP_W), 0.02),
        'gmlp_ln_b': nrm((DEPTH, GMLP_W), 0.01),
        'gmlp_ws': nrm((DEPTH, GMLP_GROUPS, CHUNK, CHUNK), CHUNK ** -0.5),
        'gmlp_bs': 1.0 + nrm((DEPTH, GMLP_GROUPS, CHUNK), 0.02),
        'sb_bias': SB_BIAS_INIT + nrm((DEPTH, SB_HEADS), 0.3),
        'w_out': nrm((DEPTH, D_MIX, D_MODEL), D_MIX ** -0.5),
        'router_g_w': nrm((DEPTH, D_MODEL, N_GROUPS), D_MODEL ** -0.5),
        'router_g_b': nrm((DEPTH, N_GROUPS), 0.01),
        'router_e_w': nrm((DEPTH, D_MODEL, N_EXPERTS), D_MODEL ** -0.5),
        'router_e_b': nrm((DEPTH, N_EXPERTS), 0.01),
        'exp_wg': nrm((DEPTH, N_EXPERTS, D_MODEL, D_EXPERT), D_MODEL ** -0.5),
        'exp_wu': nrm((DEPTH, N_EXPERTS, D_MODEL, D_EXPERT), D_MODEL ** -0.5),
        'exp_wd': nrm((DEPTH, N_EXPERTS, D_EXPERT, D_MODEL), D_EXPERT ** -0.5),
        'final_g': 1.0 + nrm((D_MODEL,), 0.02),
    }


def reference(x_prompt, x_sample, cache_k, cache_v, page_table, state_rwkv, state_rwkv_shift, c_prompt, c_sample,
              norm1_g, norm2_g, w_mod, b_mod, w_in, rwkv_mu, rwkv_w0, rwkv_w_up, rwkv_a0, rwkv_a_up, rwkv_g_up,
              rwkv_k_k, rwkv_k_a, rwkv_r_k, rwkv_ln_g, rwkv_ln_b, gmlp_ln_g, gmlp_ln_b, gmlp_ws, gmlp_bs, sb_bias,
              w_out, router_g_w, router_g_b, router_e_w, router_e_b, exp_wg, exp_wu, exp_wd, final_g):
    B = x_prompt.shape[0]
    DB = x_sample.shape[0]
    n_pages = page_table.shape[1]
    xp, xs = x_prompt, x_sample
    kp_l, vp_l, ks_l, vs_l = [], [], [], []
    Sp_l, shp_l, Ss_l, shs_l, gv_l = [], [], [], [], []
    for l in range(DEPTH):
        lw = dict(norm1_g=norm1_g[l], norm2_g=norm2_g[l], w_mod=w_mod[l], b_mod=b_mod[l], w_in=w_in[l],
                  rwkv_mu=rwkv_mu[l], rwkv_w0=rwkv_w0[l], rwkv_w_up=rwkv_w_up[l], rwkv_a0=rwkv_a0[l],
                  rwkv_a_up=rwkv_a_up[l], rwkv_g_up=rwkv_g_up[l], rwkv_k_k=rwkv_k_k[l], rwkv_k_a=rwkv_k_a[l],
                  rwkv_r_k=rwkv_r_k[l], rwkv_ln_g=rwkv_ln_g[l], rwkv_ln_b=rwkv_ln_b[l], gmlp_ln_g=gmlp_ln_g[l],
                  gmlp_ln_b=gmlp_ln_b[l], gmlp_ws=gmlp_ws[l], gmlp_bs=gmlp_bs[l], sb_bias=sb_bias[l],
                  w_out=w_out[l], router_g_w=router_g_w[l], router_g_b=router_g_b[l], router_e_w=router_e_w[l],
                  router_e_b=router_e_b[l], exp_wg=exp_wg[l], exp_wu=exp_wu[l], exp_wd=exp_wd[l])
        S0 = jnp.zeros((B, RWKV_HEADS, HEAD_DIM, HEAD_DIM), xp.dtype)
        sh0 = jnp.zeros((B, RWKV_PROJ), xp.dtype)
        xp, Sp, shp, kp, vp, _ = trunk_layer(xp, c_prompt, lw, S0, sh0, None, None)
        past_k = cache_k[l][page_table].reshape(DB, n_pages * PAGE_SIZE, SB_HEADS, HEAD_DIM)
        past_v = cache_v[l][page_table].reshape(DB, n_pages * PAGE_SIZE, SB_HEADS, HEAD_DIM)
        xs, Ss, shs, ksm, vsm, gv = trunk_layer(xs, c_sample, lw, state_rwkv[l], state_rwkv_shift[l], past_k, past_v)
        kp_l.append(kp); vp_l.append(vp); ks_l.append(ksm); vs_l.append(vsm)
        Sp_l.append(Sp); shp_l.append(shp); Ss_l.append(Ss); shs_l.append(shs); gv_l.append(gv)
    y_prompt = rms_norm(xp, final_g)
    y_sample = rms_norm(xs, final_g)
    return (y_prompt, y_sample, jnp.stack(kp_l), jnp.stack(vp_l), jnp.stack(ks_l), jnp.stack(vs_l),
            jnp.stack(Sp_l), jnp.stack(shp_l), jnp.stack(Ss_l), jnp.stack(shs_l), jnp.stack(gv_l))
```

```python
import functools

import jax
import jax.numpy as jnp
from jax import lax
from jax.experimental import pallas as pl
from jax.experimental.pallas import tpu as pltpu

F32 = jnp.float32
BF16 = jnp.bfloat16

D_MODEL = 1024
HEAD_DIM = 64
RWKV_HEADS = 4
RWKV_W = RWKV_HEADS * HEAD_DIM
LORA_W = 128
RWKV_PROJ = 3 * RWKV_W + LORA_W
W_LORA, A_LORA, G_LORA = 32, 32, 64
GN_EPS = 64e-5
GMLP_GROUPS = 4
GMLP_W = GMLP_GROUPS * HEAD_DIM
CHUNK = 128
LN_EPS = 1e-5
SB_HEADS = 8
SB_W = SB_HEADS * HEAD_DIM
PAGE_SIZE = 128
N_GROUPS = 4
EXPERTS_PER_GROUP = 4
N_EXPERTS = N_GROUPS * EXPERTS_PER_GROUP
D_EXPERT = 256
N_MOD = 6
RMS_EPS = 1e-6
SEG_WIDTHS = (RWKV_PROJ, GMLP_W, GMLP_W, SB_W, SB_W, SB_W)

LANES = 128
ROUTER_LANES = LANES
EXPERT_LANE0 = N_GROUPS
VMEM_LIMIT = 56 * 1024 * 1024

RWKV_CHUNK = 64
SB_BLOCK = 256
TOK_TILE_IN = 512
TOK_TILE_SAMPLE = 256
TOK_TILE_OUT = 512
GMLP_CHUNKS_PER_STEP = 4


def _params(sem):
    return pltpu.CompilerParams(dimension_semantics=sem, vmem_limit_bytes=VMEM_LIMIT)


def _mm(a, b, dims, hp):
    dg = lambda x, y: lax.dot_general(x, y, (dims, ((), ())), preferred_element_type=F32)
    if not hp:
        return dg(a.astype(BF16), b.astype(BF16))
    a_hi, a_lo = _split(a.astype(F32), 2)
    b_hi, b_lo = _split(b.astype(F32), 2)
    return dg(a_hi, b_hi) + dg(a_hi, b_lo) + dg(a_lo, b_hi)


def _dot(a, b, hp=False):
    return _mm(a, b, ((1,), (0,)), hp)


def _dot_nt(a, b, hp=False):
    return _mm(a, b, ((1,), (1,)), hp)


def _dot_tn(a, b, hp=False):
    return _mm(a, b, ((0,), (0,)), hp)


def _split(x, n):
    terms = []
    rem = x
    for _ in range(n):
        t = rem.astype(BF16)
        terms.append(t)
        rem = rem - t.astype(F32)
    return terms


def _dot_exact_rhs(x, m, n=2):
    out = None
    for t in _split(x, n):
        d = jnp.dot(t, m, preferred_element_type=F32)
        out = d if out is None else out + d
    return out


def _dot_exact_lhs(m, x, n=2):
    out = None
    for t in _split(x, n):
        d = jnp.dot(m, t, preferred_element_type=F32)
        out = d if out is None else out + d
    return out


def _softplus(z):
    return jnp.maximum(z, 0.0) + jnp.log1p(jnp.exp(-jnp.abs(z)))


def _sigmoid(z):
    return 1.0 / (1.0 + jnp.exp(-z))


def _gelu_tanh(x):
    return 0.5 * x * (1.0 + jnp.tanh(0.7978845608028654 * (x + 0.044715 * (x * x * x))))


def _idiv(x, n):
    assert n & (n - 1) == 0
    return jnp.right_shift(x, n.bit_length() - 1)


def _imod(x, n):
    assert n & (n - 1) == 0
    return jnp.bitwise_and(x, n - 1)


def _head_block_mask(n, dtype=BF16):
    r = _idiv(lax.broadcasted_iota(jnp.int32, (n, n), 0), HEAD_DIM)
    c = _idiv(lax.broadcasted_iota(jnp.int32, (n, n), 1), HEAD_DIM)
    return jnp.where(r == c, 1.0, 0.0).astype(dtype)


def _mod_kernel(c_ref, w_ref, b_ref, o_ref):
    c = c_ref[...]
    o_ref[0, 0] = _dot(c * _sigmoid(c), w_ref[0], hp=True) + b_ref[0, 0]


def _modulation(c_all, w_mod, b_mod):
    depth = w_mod.shape[0]
    rows = c_all.shape[0]
    return pl.pallas_call(
        _mod_kernel,
        grid=(depth, N_MOD),
        in_specs=[
            pl.BlockSpec((rows, D_MODEL), lambda l, j: (0, 0)),
            pl.BlockSpec((1, D_MODEL, D_MODEL), lambda l, j: (l, 0, j)),
            pl.BlockSpec((1, 1, 1, D_MODEL), lambda l, j: (l, j, 0, 0)),
        ],
        out_specs=pl.BlockSpec((1, 1, rows, D_MODEL), lambda l, j: (l, j, 0, 0)),
        out_shape=jax.ShapeDtypeStruct((depth, N_MOD, rows, D_MODEL), F32),
        compiler_params=_params(("arbitrary", "arbitrary")),
        name="modulation",
    )(c_all, w_mod, b_mod.reshape(depth, N_MOD, 1, D_MODEL))


def _rms_mod(x, g, scale, shift):
    y = x * lax.rsqrt(jnp.mean(x * x, axis=-1, keepdims=True) + RMS_EPS) * g
    return y * (1.0 + scale) + shift


def _inproj_kernel(x_ref, sc_ref, sh_ref, g_ref, w_ref, *o_refs, hp):
    nb, tt, d = x_ref.shape
    h = _rms_mod(x_ref[...], g_ref[...], sc_ref[...], sh_ref[...]).reshape(nb * tt, d)
    if not hp:
        h = h.astype(BF16)
    off = 0
    for o_ref in o_refs:
        w = o_ref.shape[-1]
        o_ref[...] = _dot(h, w_ref[:, off:off + w], hp).reshape(nb, tt, w)
        off += w


def _in_projection(x, scale, shift, g, w_in, nb, tt, hp):
    B, T, _ = x.shape
    p_total = w_in.shape[1]
    tok = lambda w: pl.BlockSpec((nb, tt, w), lambda i, j: (i, j, 0))
    per_seq = pl.BlockSpec((nb, 1, D_MODEL), lambda i, j: (i, 0, 0))
    return pl.pallas_call(
        functools.partial(_inproj_kernel, hp=hp),
        grid=(B // nb, T // tt),
        in_specs=[tok(D_MODEL), per_seq, per_seq,
                  pl.BlockSpec((1, 1, D_MODEL), lambda i, j: (0, 0, 0)),
                  pl.BlockSpec((D_MODEL, p_total), lambda i, j: (0, 0))],
        out_specs=[tok(w) for w in SEG_WIDTHS],
        out_shape=[jax.ShapeDtypeStruct((B, T, w), F32) for w in SEG_WIDTHS],
        compiler_params=_params(("arbitrary", "arbitrary")),
        name="in_projection",
    )(x, scale, shift, g.reshape(1, 1, D_MODEL), w_in)


def _rwkv_kernel(p_ref, prev0_ref, s0_ref, mu_ref, w0_ref, wup_ref, a0_ref, aup_ref, gup_ref, kk_ref, ka_ref,
                 rk_ref, lng_ref, lnb_ref, y_ref, s_out_ref, s_scr, carry_scr, *, ns, ts, hp):
    C = ns * ts
    W = RWKV_W
    chunk = pl.program_id(1)
    dot = functools.partial(_dot, hp=hp)
    dot_nt = functools.partial(_dot_nt, hp=hp)
    dot_tn = functools.partial(_dot_tn, hp=hp)

    @pl.when(chunk == 0)
    def _():
        s_scr[...] = s0_ref[...]
        carry_scr[...] = prev0_ref[...]

    p = p_ref[...].reshape(C, RWKV_PROJ)
    row1 = lax.broadcasted_iota(jnp.int32, (C, 1), 0)
    prev_first = jnp.broadcast_to(carry_scr[...], (ns, ts, RWKV_PROJ)).reshape(C, RWKV_PROJ)
    prev = jnp.where(_imod(row1, ts) == 0, prev_first, pltpu.roll(p, 1, 0))
    carry_scr[...] = p_ref[:, ts - 1:ts, :]
    ps = p + mu_ref[...] * (prev - p)

    r = ps[:, 0:W]
    k = ps[:, W:2 * W]
    v = ps[:, 2 * W:3 * W]
    lora = ps[:, 3 * W:]
    w = -_softplus(-(w0_ref[...] + dot(jnp.tanh(lora), wup_ref[...]))) - 0.5
    lw = -jnp.exp(w)
    a = _sigmoid(a0_ref[...] + dot(lora, aup_ref[...]))
    g = dot(_sigmoid(lora), gup_ref[...])

    head_bd = _head_block_mask(W)
    kkr = k * kk_ref[...]
    kk = kkr * lax.rsqrt(jnp.maximum(_dot_exact_rhs(kkr * kkr, head_bd), 1e-24))
    kh = k * (1.0 + (a - 1.0) * ka_ref[...])
    a_vec = -kk
    b_vec = kk * a

    rowi = lax.broadcasted_iota(jnp.int32, (C, C), 0)
    coli = lax.broadcasted_iota(jnp.int32, (C, C), 1)
    same = _idiv(rowi, ts) == _idiv(coli, ts)
    strict = jnp.logical_and(same, coli < rowi)
    incl = jnp.logical_and(same, coli <= rowi)
    L = _dot_exact_lhs(jnp.where(incl, 1.0, 0.0).astype(BF16), lw, 3)
    L_end = _dot_exact_lhs(jnp.where(same, 1.0, 0.0).astype(BF16), lw, 3)
    e_neg = jnp.exp(-L)
    e_end = jnp.exp(L_end - L)
    At = a_vec * jnp.exp(L - lw)
    Rt = r * jnp.exp(L)
    Bt = b_vec * e_neg
    Kt = kh * e_neg
    Bh = b_vec * e_end
    Kh = kh * e_end
    P_end = jnp.exp(L_end)

    lane_head = _idiv(lax.broadcasted_iota(jnp.int32, (1, W), 1), HEAD_DIM)
    eye =jnp.where(rowi == coli, 1.0, 0.0)
    n_factors = max(1, (ts - 1).bit_length())
    T_heads, rb_heads = [], []
    akv = jnp.zeros((C, W), F32)
    rkv = jnp.zeros((C, W), F32)
    for h in range(RWKV_HEADS):
        mh = lane_head == h
        At_h = jnp.where(mh, At, 0.0)
        Rt_h = jnp.where(mh, Rt, 0.0)
        ab = jnp.where(strict, dot_nt(At_h, Bt), 0.0)
        ak = jnp.where(strict, dot_nt(At_h, Kt), 0.0)
        rb = jnp.where(incl, dot_nt(Rt_h, Bt), 0.0)
        rk = jnp.where(incl, dot_nt(Rt_h, Kt), 0.0)
        T = eye + ab
        apow = ab
        for _ in range(n_factors - 1):
            apow = dot(apow, apow)
            T = T + dot(T, apow)
        akv = akv + jnp.where(mh, dot(ak, v), 0.0)
        rkv = rkv + jnp.where(mh, dot(rk, v), 0.0)
        T_heads.append(T)
        rb_heads.append(rb)

    xs, ys = [], []
    for q in range(ns):
        rows = slice(q * ts, (q + 1) * ts)
        Sq = s_scr[q]
        xs.append(dot_nt(At[rows], Sq))
        ys.append(dot_nt(Rt[rows], Sq))
    X = xs[0] if ns == 1 else jnp.concatenate(xs, axis=0)
    YS = ys[0] if ns == 1 else jnp.concatenate(ys, axis=0)
    rhs = X + akv
    U = jnp.zeros((C, W), F32)
    for h in range(RWKV_HEADS):
        U = U + jnp.where(lane_head == h, dot(T_heads[h], rhs), 0.0)
    Y = YS + rkv
    for h in range(RWKV_HEADS):
        Y = Y + jnp.where(lane_head == h, dot(rb_heads[h], U), 0.0)

    bd_f32 = _head_block_mask(W, F32)
    for q in range(ns):
        rows = slice(q * ts, (q + 1) * ts)
        upd = dot_tn(U[rows], Bh[rows]) + dot_tn(v[rows], Kh[rows])
        s_scr[q] = (s_scr[q] * P_end[q * ts:q * ts + 1, :] + upd) * bd_f32

    inv_n = 1.0 / HEAD_DIM
    mean = _dot_exact_rhs(Y, head_bd) * inv_n
    yc = Y - mean
    var = _dot_exact_rhs(yc * yc, head_bd) * inv_n
    yn = yc * lax.rsqrt(var + GN_EPS) * lng_ref[...] + lnb_ref[...]
    bonus = _dot_exact_rhs(r * kh * rk_ref[...], head_bd) * v
    y_ref[...] = ((yn + bonus) * g).reshape(ns, ts, W)

    @pl.when(chunk == pl.num_programs(1) - 1)
    def _():
        s_out_ref[...] = s_scr[...]


def _rwkv_mix(p_rwkv, prev0, s0_full, lw, ns, ts, hp):
    B, T, _ = p_rwkv.shape
    W = RWKV_W
    vec = lambda n: pl.BlockSpec((1, n), lambda i, j: (0, 0))
    mat = lambda: pl.BlockSpec((LORA_W, W), lambda i, j: (0, 0))
    kern = functools.partial(_rwkv_kernel, ns=ns, ts=ts, hp=hp)
    return pl.pallas_call(
        kern,
        grid=(B // ns, T // ts),
        in_specs=[
            pl.BlockSpec((ns, ts, RWKV_PROJ), lambda i, j: (i, j, 0)),
            pl.BlockSpec((ns, 1, RWKV_PROJ), lambda i, j: (i, 0, 0)),
            pl.BlockSpec((ns, W, W), lambda i, j: (i, 0, 0)),
            vec(RWKV_PROJ), vec(W), mat(), vec(W), mat(), mat(), vec(W), vec(W), vec(W), vec(W), vec(W),
        ],
        out_specs=[
            pl.BlockSpec((ns, ts, W), lambda i, j: (i, j, 0)),
            pl.BlockSpec((ns, W, W), lambda i, j: (i, 0, 0)),
        ],
        out_shape=[jax.ShapeDtypeStruct((B, T, W), F32), jax.ShapeDtypeStruct((B, W, W), F32)],
        scratch_shapes=[pltpu.VMEM((ns, W, W), F32), pltpu.VMEM((ns, 1, RWKV_PROJ), F32)],
        compiler_params=_params(("arbitrary", "arbitrary")),
        name="rwkv_mix",
    )(p_rwkv, prev0, s0_full, lw['mu'], lw['w0'], lw['w_up'], lw['a0'], lw['a_up'], lw['g_up'],
      lw['k_k'], lw['k_a'], lw['r_k'], lw['ln_g'], lw['ln_b'])


def _state_to_full(s):
    B = s.shape[0]
    eye = jnp.eye(RWKV_HEADS, dtype=s.dtype)
    return jnp.einsum('bhij,hg->bhigj', s, eye).reshape(B, RWKV_W, RWKV_W)


def _state_from_full(sf):
    B = sf.shape[0]
    s5 = sf.reshape(B, RWKV_HEADS, HEAD_DIM, RWKV_HEADS, HEAD_DIM)
    return jnp.stack([s5[:, h, :, h, :] for h in range(RWKV_HEADS)], axis=1)


def _gmlp_kernel(u_ref, v_ref, lng_ref, lnb_ref, ws_ref, bias_ref, y_ref, vrow_ref, *, hp):
    W = GMLP_W
    head_bd = _head_block_mask(W)
    inv_n = 1.0 / HEAD_DIM
    row = lax.broadcasted_iota(jnp.int32, (CHUNK, CHUNK), 0)
    col = lax.broadcasted_iota(jnp.int32, (CHUNK, CHUNK), 1)
    tril = col <= row
    lane_group = _idiv(lax.broadcasted_iota(jnp.int32, (1, W), 1), HEAD_DIM)
    wms = [jnp.where(tril, ws_ref[gi], 0.0) for gi in range(GMLP_GROUPS)]
    if not hp:
        wms = [wm.astype(BF16) for wm in wms]
    for ci in range(u_ref.shape[0]):
        u = _gelu_tanh(u_ref[ci])
        v = _gelu_tanh(v_ref[ci])
        mean = _dot_exact_rhs(v, head_bd) * inv_n
        vc = v - mean
        var = _dot_exact_rhs(vc * vc, head_bd) * inv_n
        vn = vc * lax.rsqrt(var + LN_EPS) * lng_ref[...] + lnb_ref[...]
        vrow_ref[ci] = vn
        mixed = bias_ref[...]
        for gi in range(GMLP_GROUPS):
            mixed = mixed + _dot(wms[gi], jnp.where(lane_group == gi, vn, 0.0), hp)
        y_ref[ci] = u * mixed


def _gmlp_mix(u, v, ln_g, ln_b, ws_chunk, bias_chunk, hp):
    n = u.shape[0]
    nc = GMLP_CHUNKS_PER_STEP
    blk = pl.BlockSpec((nc, CHUNK, GMLP_W), lambda i: (i, 0, 0))
    vec = pl.BlockSpec((1, GMLP_W), lambda i: (0, 0))
    return pl.pallas_call(
        functools.partial(_gmlp_kernel, hp=hp),
        grid=(n // nc,),
        in_specs=[blk, blk, vec, vec,
                  pl.BlockSpec((GMLP_GROUPS, CHUNK, CHUNK), lambda i: (0, 0, 0)),
                  pl.BlockSpec((CHUNK, GMLP_W), lambda i: (0, 0))],
        out_specs=[blk, blk],
        out_shape=[jax.ShapeDtypeStruct(u.shape, F32), jax.ShapeDtypeStruct(u.shape, F32)],
        compiler_params=_params(("arbitrary",)),
        name="gmlp_mix",
    )(u, v, ln_g, ln_b, ws_chunk, bias_chunk)


def _sb_block(qb, kb, vb, bias, U, carry, acc, mask, hp=False):
    z = _dot_nt(qb, kb, hp) + bias
    sp = _softplus(z)
    log_not = -sp
    if mask is not None:
        log_not = jnp.where(mask, log_not, 0.0)
    tail = _dot_exact_rhs(log_not, U, 2) + carry
    a = jnp.exp(z - sp + tail)
    if mask is not None:
        a = jnp.where(mask, a, 0.0)
    acc = acc + _dot(a, vb, hp)
    carry = carry + jnp.sum(log_not, axis=-1, keepdims=True)
    return carry, acc


def _suffix_matrix(n):
    r = lax.broadcasted_iota(jnp.int32, (n, n), 0)
    c = lax.broadcasted_iota(jnp.int32, (n, n), 1)
    return jnp.where(r > c, 1.0, 0.0).astype(BF16)


def _sb_prompt_kernel(bias_ref, q_ref, k_ref, v_ref, o_ref, *, blk):
    h = pl.program_id(1)
    qi = pl.program_id(2)
    bias = bias_ref[h]
    qb = (q_ref[0, 0] * (HEAD_DIM ** -0.5)).astype(BF16)
    U = _suffix_matrix(blk)
    row = lax.broadcasted_iota(jnp.int32, (blk, blk), 0)
    col = lax.broadcasted_iota(jnp.int32, (blk, blk), 1)

    def kv(j):
        start = pl.multiple_of(j * blk, blk)
        return k_ref[0, 0, pl.ds(start, blk), :].astype(BF16), v_ref[0, 0, pl.ds(start, blk), :].astype(BF16)

    kb, vb = kv(qi)
    carry = jnp.zeros((blk, 1), F32)
    acc = jnp.zeros((blk, HEAD_DIM), F32)
    carry, acc = _sb_block(qb, kb, vb, bias, U, carry, acc, col < row)

    def body(i, st):
        kb, vb = kv(qi - 1 - i)
        return _sb_block(qb, kb, vb, bias, U, st[0], st[1], None)

    carry, acc = lax.fori_loop(0, qi, body, (carry, acc))
    o_ref[0, 0] = acc


def _sb_prompt(q, k, v, bias):
    B, H, T, _ = q.shape
    blk = SB_BLOCK
    qspec = pl.BlockSpec((1, 1, blk, HEAD_DIM), lambda b, h, i: (b, h, i, 0))
    kvspec = pl.BlockSpec((1, 1, T, HEAD_DIM), lambda b, h, i: (b, h, 0, 0))
    return pl.pallas_call(
        functools.partial(_sb_prompt_kernel, blk=blk),
        grid=(B, H, T // blk),
        in_specs=[pl.BlockSpec(memory_space=pltpu.SMEM), qspec, kvspec, kvspec],
        out_specs=qspec,
        out_shape=jax.ShapeDtypeStruct(q.shape, F32),
        compiler_params=_params(("arbitrary", "arbitrary", "arbitrary")),
        name="sb_prompt",
    )(bias, q, k, v)


def _sb_sample_kernel(pt_ref, q_ref, kn_ref, vn_ref, bias_ref, *refs, n_pages):
    k_pages = refs[:n_pages]
    v_pages = refs[n_pages:2 * n_pages]
    o_ref = refs[2 * n_pages]
    T = q_ref.shape[1]
    R = SB_HEADS * T
    q = q_ref[0] * (HEAD_DIM ** -0.5)
    q_rows = jnp.concatenate([q] * SB_HEADS, axis=0)
    row_head = _idiv(lax.broadcasted_iota(jnp.int32, (R, SB_W), 0), T)
    lane_head = _idiv(lax.broadcasted_iota(jnp.int32, (R, SB_W), 1), HEAD_DIM)
    own = row_head == lane_head
    qbd = jnp.where(own, q_rows, 0.0)
    U = _suffix_matrix(PAGE_SIZE)
    bias = bias_ref[...]

    pad = jnp.zeros((PAGE_SIZE - T, SB_W), F32)
    k_new = jnp.concatenate([kn_ref[0], pad], axis=0)
    v_new = jnp.concatenate([vn_ref[0], pad], axis=0)
    tok = _imod(lax.broadcasted_iota(jnp.int32, (R, PAGE_SIZE), 0), T)
    key = lax.broadcasted_iota(jnp.int32, (R, PAGE_SIZE), 1)
    carry = jnp.zeros((R, 1), F32)
    acc = jnp.zeros((R, SB_W), F32)
    carry, acc = _sb_block(qbd, k_new, v_new, bias, U, carry, acc, key < tok, hp=True)
    for j in reversed(range(n_pages)):
        carry, acc = _sb_block(qbd, k_pages[j][0, 0], v_pages[j][0, 0], bias, U, carry, acc, None, hp=True)
    acc = jnp.where(own, acc, 0.0)
    out = acc[0:T]
    for h in range(1, SB_HEADS):
        out = out + acc[h * T:(h + 1) * T]
    o_ref[0] = out


def _sb_sample(q, k_new, v_new, cache_k, cache_v, page_table_flat, bias_rows, layer, n_pages):
    DB, T, _ = q.shape
    tok = pl.BlockSpec((1, T, SB_W), lambda b, pt: (b, 0, 0))

    def page_spec(j):
        return pl.BlockSpec((1, 1, PAGE_SIZE, SB_W), lambda b, pt, j=j: (layer, pt[b * n_pages + j], 0, 0))

    grid_spec = pltpu.PrefetchScalarGridSpec(
        num_scalar_prefetch=1,
        grid=(DB,),
        in_specs=[tok, tok, tok, pl.BlockSpec((SB_HEADS * T, PAGE_SIZE), lambda b, pt: (0, 0))]
        + [page_spec(j) for j in range(n_pages)] * 2,
        out_specs=tok,
    )
    return pl.pallas_call(
        functools.partial(_sb_sample_kernel, n_pages=n_pages),
        grid_spec=grid_spec,
        out_shape=jax.ShapeDtypeStruct(q.shape, F32),
        compiler_params=_params(("arbitrary",)),
        name="sb_sample",
    )(page_table_flat, q, k_new, v_new, bias_rows, *([cache_k] * n_pages), *([cache_v] * n_pages))


def _route(logits):
    lane_i = lax.broadcasted_iota(jnp.int32, logits.shape, 1)
    lane = lane_i.astype(F32)
    neg = -jnp.inf
    none = float(ROUTER_LANES)
    is_g = lane_i < N_GROUPS
    lg = jnp.where(is_g, logits, neg)
    mg = jnp.max(lg, axis=-1, keepdims=True)
    pg_top = 1.0 / jnp.sum(jnp.where(is_g, jnp.exp(lg - mg), 0.0), axis=-1, keepdims=True)
    g_idx = jnp.min(jnp.where(lg == mg, lane, none), axis=-1, keepdims=True)
    e_rel = lane_i - EXPERT_LANE0
    lane_group = _idiv(e_rel, EXPERTS_PER_GROUP).astype(F32)
    in_group = jnp.logical_and(jnp.logical_and(e_rel >= 0, e_rel < N_EXPERTS), lane_group == g_idx)
    el = jnp.where(in_group, logits, neg)
    m1 = jnp.max(el, axis=-1, keepdims=True)
    i1 = jnp.min(jnp.where(el == m1, lane, none), axis=-1, keepdims=True)
    el2 = jnp.where(lane == i1, neg, el)
    m2 = jnp.max(el2, axis=-1, keepdims=True)
    i2 = jnp.min(jnp.where(el2 == m2, lane, none), axis=-1, keepdims=True)
    t = jnp.exp(m2 - m1)
    w1 = 1.0 / (1.0 + t)
    w2 = t / (1.0 + t)
    return jnp.where(lane == i1, pg_top * w1, 0.0) + jnp.where(lane == i2, pg_top * w2, 0.0)


def _outmoe_kernel(x_ref, yr_ref, yg_ref, ys_ref, g1_ref, sc2_ref, sh2_ref, g2_ref, n2g_ref, fg_ref, wout_ref,
                   wr_ref, br_ref, wg_ref, wu_ref, wd_ref, o_ref, x1_scr, h2_scr, dw_scr, acc_scr, *, final, hp):
    nb, tt, d = x_ref.shape
    rows = nb * tt
    e = pl.program_id(2)

    @pl.when(e == 0)
    def _():
        flat = lambda ref: ref[...].reshape(rows, ref.shape[-1])
        mixed = (_dot(flat(yr_ref), wout_ref[0:RWKV_W], hp)
                 + _dot(flat(yg_ref), wout_ref[RWKV_W:RWKV_W + GMLP_W], hp)
                 + _dot(flat(ys_ref), wout_ref[RWKV_W + GMLP_W:], hp))
        x1 = x_ref[...] + g1_ref[...] * mixed.reshape(nb, tt, d)
        h2 = _rms_mod(x1, n2g_ref[...], sc2_ref[...], sh2_ref[...]).reshape(rows, d)
        x1_scr[...] = x1
        h2_scr[...] = h2.astype(BF16)
        logits = jnp.dot(h2, wr_ref[...], preferred_element_type=F32, precision=lax.Precision.HIGHEST) + br_ref[...]
        dw_scr[...] = _route(logits)
        acc_scr[...] = jnp.zeros_like(acc_scr)

    hb = h2_scr[...]
    gate = jnp.dot(hb, wg_ref[0, 0], preferred_element_type=F32)
    up = jnp.dot(hb, wu_ref[0, 0], preferred_element_type=F32)
    lane = lax.broadcasted_iota(jnp.int32, (rows, ROUTER_LANES), 1)
    w_e = jnp.sum(jnp.where(lane == e + EXPERT_LANE0, dw_scr[...], 0.0), axis=-1, keepdims=True)
    hid = gate * _sigmoid(gate) * up * w_e
    acc_scr[...] += jnp.dot(hid.astype(BF16), wd_ref[0, 0], preferred_element_type=F32)

    @pl.when(e == pl.num_programs(2) - 1)
    def _():
        x2 = x1_scr[...] + g2_ref[...] * acc_scr[...].reshape(nb, tt, d)
        if final:
            x2 = x2 * lax.rsqrt(jnp.mean(x2 * x2, axis=-1, keepdims=True) + RMS_EPS) * fg_ref[...]
        o_ref[...] = x2


def _out_moe(x, y_rwkv, y_gmlp, y_sb, gate1, scale2, shift2, gate2, lw, layer, final_g, final, nb, tt, hp):
    B, T, _ = x.shape
    rows = nb * tt
    tok = lambda w: pl.BlockSpec((nb, tt, w), lambda i, j, e: (i, j, 0))
    per_seq = pl.BlockSpec((nb, 1, D_MODEL), lambda i, j, e: (i, 0, 0))
    vec = pl.BlockSpec((1, 1, D_MODEL), lambda i, j, e: (0, 0, 0))
    return pl.pallas_call(
        functools.partial(_outmoe_kernel, final=final, hp=hp),
        grid=(B // nb, T // tt, N_EXPERTS),
        in_specs=[tok(D_MODEL), tok(RWKV_W), tok(GMLP_W), tok(SB_W), per_seq, per_seq, per_seq, per_seq, vec, vec,
                  pl.BlockSpec((D_MODEL, D_MODEL), lambda i, j, e: (0, 0)),
                  pl.BlockSpec((D_MODEL, ROUTER_LANES), lambda i, j, e: (0, 0)),
                  pl.BlockSpec((1, ROUTER_LANES), lambda i, j, e: (0, 0)),
                  pl.BlockSpec((1, 1, D_MODEL, D_EXPERT), lambda i, j, e: (layer, e, 0, 0)),
                  pl.BlockSpec((1, 1, D_MODEL, D_EXPERT), lambda i, j, e: (layer, e, 0, 0)),
                  pl.BlockSpec((1, 1, D_EXPERT, D_MODEL), lambda i, j, e: (layer, e, 0, 0))],
        out_specs=tok(D_MODEL),
        out_shape=jax.ShapeDtypeStruct(x.shape, F32),
        scratch_shapes=[pltpu.VMEM((nb, tt, D_MODEL), F32), pltpu.VMEM((rows, D_MODEL), BF16),
                        pltpu.VMEM((rows, ROUTER_LANES), F32), pltpu.VMEM((rows, D_MODEL), F32)],
        compiler_params=_params(("arbitrary", "arbitrary", "arbitrary")),
        name="out_moe",
    )(x, y_rwkv, y_gmlp, y_sb, gate1, scale2, shift2, gate2, lw['norm2_g'], final_g,
      lw['w_out_f32'] if hp else lw['w_out'], lw['w_router'], lw['b_router'], lw['exp_wg'], lw['exp_wu'], lw['exp_wd'])


def _trunk_layer(x, mod, lw, layer, s0_full, prev0, gm_ws, gm_bias, sb_fn, final_g, final, nb_tok_in, tt_in,
                 nb_tok_out, tt_out, rwkv_ns, rwkv_ts, hp):
    B, T, _ = x.shape
    shift1, scale1, gate1, shift2, scale2, gate2 = mod
    p_rwkv, g_u, g_v, q, k, v = _in_projection(x, scale1, shift1, lw['norm1_g'],
                                               lw['w_in_f32'] if hp else lw['w_in'], nb_tok_in, tt_in, hp)
    y_rwkv, s_new = _rwkv_mix(p_rwkv, prev0, s0_full, lw['rwkv'], rwkv_ns, rwkv_ts, hp)
    n_chunks = B * T // CHUNK
    y_gmlp, v_rows = _gmlp_mix(g_u.reshape(n_chunks, CHUNK, GMLP_W), g_v.reshape(n_chunks, CHUNK, GMLP_W),
                               lw['gmlp_ln_g'], lw['gmlp_ln_b'], gm_ws, gm_bias, hp)
    y_gmlp = y_gmlp.reshape(B, T, GMLP_W)
    v_rows = v_rows.reshape(B, T, GMLP_W)
    y_sb = sb_fn(q, k, v)
    x_out = _out_moe(x, y_rwkv, y_gmlp, y_sb, gate1, scale2, shift2, gate2, lw, layer, final_g, final,
                     nb_tok_out, tt_out, hp)
    return x_out, s_new, p_rwkv, k, v, v_rows


def _pad_rows(w, row0):
    return jnp.zeros((LORA_W, RWKV_W), F32).at[row0:row0 + w.shape[0]].set(w)


def kernel(x_prompt, x_sample, cache_k, cache_v, page_table, state_rwkv, state_rwkv_shift, c_prompt, c_sample, norm1_g, norm2_g, w_mod, b_mod, w_in, rwkv_mu, rwkv_w0, rwkv_w_up, rwkv_a0, rwkv_a_up, rwkv_g_up, rwkv_k_k, rwkv_k_a, rwkv_r_k, rwkv_ln_g, rwkv_ln_b, gmlp_ln_g, gmlp_ln_b, gmlp_ws, gmlp_bs, sb_bias, w_out, router_g_w, router_g_b, router_e_w, router_e_b, exp_wg, exp_wu, exp_wd, final_g):
    B, T, _ = x_prompt.shape
    DB, TS, _ = x_sample.shape
    depth = w_in.shape[0]
    n_pages = page_table.shape[1]
    n_pool = cache_k.shape[1]

    mod_all = _modulation(jnp.concatenate([c_prompt, c_sample], axis=0), w_mod, b_mod)
    cache_k4 = cache_k.reshape(depth, n_pool, PAGE_SIZE, SB_W)
    cache_v4 = cache_v.reshape(depth, n_pool, PAGE_SIZE, SB_W)
    pt_flat = page_table.reshape(-1)
    exp_wg_b = exp_wg.astype(BF16)
    exp_wu_b = exp_wu.astype(BF16)
    exp_wd_b = exp_wd.astype(BF16)
    fg = final_g.reshape(1, 1, D_MODEL)
    seqs_per_chunk = CHUNK // TS

    xp, xs = x_prompt, x_sample
    outs = {n: [] for n in ('kp', 'vp', 'ks', 'vs', 'Sp', 'shp', 'Ss', 'shs', 'gv')}
    for l in range(depth):
        w_router = jnp.zeros((D_MODEL, ROUTER_LANES), F32)
        w_router = w_router.at[:, :N_GROUPS].set(router_g_w[l]).at[:, EXPERT_LANE0:EXPERT_LANE0 + N_EXPERTS].set(router_e_w[l])
        b_router = jnp.zeros((1, ROUTER_LANES), F32)
        b_router = b_router.at[0, :N_GROUPS].set(router_g_b[l]).at[0, EXPERT_LANE0:EXPERT_LANE0 + N_EXPERTS].set(router_e_b[l])
        row = lambda a: a.reshape(1, -1)
        lw = dict(
            norm1_g=norm1_g[l], norm2_g=norm2_g[l].reshape(1, 1, D_MODEL), w_in=w_in[l].astype(BF16),
            w_in_f32=w_in[l], w_out=w_out[l].astype(BF16), w_out_f32=w_out[l],
            w_router=w_router, b_router=b_router,
            exp_wg=exp_wg_b, exp_wu=exp_wu_b, exp_wd=exp_wd_b,
            gmlp_ln_g=row(gmlp_ln_g[l]), gmlp_ln_b=row(gmlp_ln_b[l]),
            rwkv=dict(mu=row(rwkv_mu[l]), w0=row(rwkv_w0[l]), w_up=_pad_rows(rwkv_w_up[l], 0), a0=row(rwkv_a0[l]),
                      a_up=_pad_rows(rwkv_a_up[l], W_LORA), g_up=_pad_rows(rwkv_g_up[l], W_LORA + A_LORA),
                      k_k=row(rwkv_k_k[l]), k_a=row(rwkv_k_a[l]), r_k=row(rwkv_r_k[l]), ln_g=row(rwkv_ln_g[l]),
                      ln_b=row(rwkv_ln_b[l])),
        )
        final = l == depth - 1
        mods_p = [mod_all[l, i, :B].reshape(B, 1, D_MODEL) for i in range(N_MOD)]
        mods_s = [mod_all[l, i, B:].reshape(DB, 1, D_MODEL) for i in range(N_MOD)]

        bias_p = jnp.repeat(jnp.swapaxes(gmlp_bs[l], 0, 1), HEAD_DIM, axis=1)

        def sb_prompt(q, k, v, l=l):
            hm = lambda a: jnp.transpose(a.reshape(B, T, SB_HEADS, HEAD_DIM), (0, 2, 1, 3))
            o = _sb_prompt(hm(q), hm(k), hm(v), sb_bias[l])
            return jnp.transpose(o, (0, 2, 1, 3)).reshape(B, T, SB_W)

        xp, Sp, pp, kp, vp, _ = _trunk_layer(
            xp, mods_p, lw, l, jnp.zeros((B, RWKV_W, RWKV_W), F32), jnp.zeros((B, 1, RWKV_PROJ), F32),
            gmlp_ws[l], bias_p, sb_prompt, fg, final, 1, TOK_TILE_IN, 1, TOK_TILE_OUT, 1, RWKV_CHUNK, False)

        ws_s = jnp.einsum('ab,gts->gatbs', jnp.eye(seqs_per_chunk, dtype=F32), gmlp_ws[l][:, :TS, :TS])
        ws_s = ws_s.reshape(GMLP_GROUPS, CHUNK, CHUNK)
        bias_s = jnp.tile(bias_p[:TS], (seqs_per_chunk, 1))
        bias_rows = jnp.broadcast_to(jnp.repeat(sb_bias[l], TS)[:, None], (SB_HEADS * TS, PAGE_SIZE))

        def sb_sample(q, k, v, l=l, bias_rows=bias_rows):
            return _sb_sample(q, k, v, cache_k4, cache_v4, pt_flat, bias_rows, l, n_pages)

        xs, Ss, ps_, ksm, vsm, gv = _trunk_layer(
            xs, mods_s, lw, l, _state_to_full(state_rwkv[l]), state_rwkv_shift[l].reshape(DB, 1, RWKV_PROJ),
            ws_s, bias_s, sb_sample, fg, final, TOK_TILE_SAMPLE // TS, TS, TOK_TILE_OUT // TS, TS,
            RWKV_CHUNK // TS, TS, True)

        outs['kp'].append(kp.reshape(B, T, SB_HEADS, HEAD_DIM))
        outs['vp'].append(vp.reshape(B, T, SB_HEADS, HEAD_DIM))
        outs['ks'].append(ksm.reshape(DB, TS, SB_HEADS, HEAD_DIM))
        outs['vs'].append(vsm.reshape(DB, TS, SB_HEADS, HEAD_DIM))
        outs['Sp'].append(_state_from_full(Sp))
        outs['shp'].append(pp[:, -1])
        outs['Ss'].append(_state_from_full(Ss))
        outs['shs'].append(ps_[:, -1])
        outs['gv'].append(gv)

    st = lambda n: jnp.stack(outs[n])
    return (xp, xs, st('kp'), st('vp'), st('ks'), st('vs'), st('Sp'), st('shp'), st('Ss'), st('shs'), st('gv'))
```

```python
import functools

import jax
import jax.numpy as jnp
from jax import lax
from jax.experimental import pallas as pl
from jax.experimental.pallas import tpu as pltpu

F32 = jnp.float32
BF16 = jnp.bfloat16

D_MODEL = 1024
HEAD_DIM = 64
RWKV_HEADS = 4
RWKV_W = RWKV_HEADS * HEAD_DIM
LORA_W = 128
RWKV_PROJ = 3 * RWKV_W + LORA_W
W_LORA, A_LORA, G_LORA = 32, 32, 64
GN_EPS = 64e-5
GMLP_GROUPS = 4
GMLP_W = GMLP_GROUPS * HEAD_DIM
CHUNK = 128
LN_EPS = 1e-5
SB_HEADS = 8
SB_W = SB_HEADS * HEAD_DIM
PAGE_SIZE = 128
N_GROUPS = 4
EXPERTS_PER_GROUP = 4
N_EXPERTS = N_GROUPS * EXPERTS_PER_GROUP
D_EXPERT = 256
N_MOD = 6
RMS_EPS = 1e-6
SEG_WIDTHS = (RWKV_PROJ, GMLP_W, GMLP_W, SB_W, SB_W, SB_W)

LANES = 128
ROUTER_LANES = LANES
EXPERT_LANE0 = N_GROUPS
VMEM_LIMIT = 56 * 1024 * 1024

RWKV_CHUNK = 64
SB_BLOCK = 256
SB_STRIP = 64
RWKV_SEQS_PER_STEP = 4
SB_PAGES_PER_STEP = 8
TOK_TILE_IN = 512
TOK_TILE_SAMPLE = 256
TOK_TILE_OUT = 512
GMLP_CHUNKS_PER_STEP = 4


def _params(sem):
    return pltpu.CompilerParams(dimension_semantics=sem, vmem_limit_bytes=VMEM_LIMIT)


def _mm(a, b, dims, hp):
    dg = lambda x, y: lax.dot_general(x, y, (dims, ((), ())), preferred_element_type=F32)
    if not hp:
        return dg(a.astype(BF16), b.astype(BF16))
    a_hi, a_lo = _split(a.astype(F32), 2)
    b_hi, b_lo = _split(b.astype(F32), 2)
    return dg(a_hi, b_hi) + dg(a_hi, b_lo) + dg(a_lo, b_hi)


def _dot(a, b, hp=False):
    return _mm(a, b, ((1,), (0,)), hp)


def _dot_nt(a, b, hp=False):
    return _mm(a, b, ((1,), (1,)), hp)


def _dot_tn(a, b, hp=False):
    return _mm(a, b, ((0,), (0,)), hp)


def _split(x, n):
    terms = []
    rem = x
    for _ in range(n):
        t = rem.astype(BF16)
        terms.append(t)
        rem = rem - t.astype(F32)
    return terms


def _dot_exact_rhs(x, m, n=2):
    out = None
    for t in _split(x, n):
        d = jnp.dot(t, m, preferred_element_type=F32)
        out = d if out is None else out + d
    return out


def _dot_exact_lhs(m, x, n=2):
    out = None
    for t in _split(x, n):
        d = jnp.dot(m, t, preferred_element_type=F32)
        out = d if out is None else out + d
    return out


def _softplus(z):
    return jnp.maximum(z, 0.0) + jnp.log1p(jnp.exp(-jnp.abs(z)))


def _sigmoid(z):
    return 1.0 / (1.0 + jnp.exp(-z))


def _gelu_tanh(x):
    return 0.5 * x * (1.0 + jnp.tanh(0.7978845608028654 * (x + 0.044715 * (x * x * x))))


def _idiv(x, n):
    assert n & (n - 1) == 0
    return jnp.right_shift(x, n.bit_length() - 1)


def _imod(x, n):
    assert n & (n - 1) == 0
    return jnp.bitwise_and(x, n - 1)


def _head_block_mask(n, dtype=BF16):
    r = _idiv(lax.broadcasted_iota(jnp.int32, (n, n), 0), HEAD_DIM)
    c = _idiv(lax.broadcasted_iota(jnp.int32, (n, n), 1), HEAD_DIM)
    return jnp.where(r == c, 1.0, 0.0).astype(dtype)


def _mod_kernel(c_ref, w_ref, b_ref, o_ref):
    c = c_ref[...]
    o_ref[0, 0] = _dot(c * _sigmoid(c), w_ref[0], hp=True) + b_ref[0, 0]


def _modulation(c_all, w_mod, b_mod):
    depth = w_mod.shape[0]
    rows = c_all.shape[0]
    return pl.pallas_call(
        _mod_kernel,
        grid=(depth, N_MOD),
        in_specs=[
            pl.BlockSpec((rows, D_MODEL), lambda l, j: (0, 0)),
            pl.BlockSpec((1, D_MODEL, D_MODEL), lambda l, j: (l, 0, j)),
            pl.BlockSpec((1, 1, 1, D_MODEL), lambda l, j: (l, j, 0, 0)),
        ],
        out_specs=pl.BlockSpec((1, 1, rows, D_MODEL), lambda l, j: (l, j, 0, 0)),
        out_shape=jax.ShapeDtypeStruct((depth, N_MOD, rows, D_MODEL), F32),
        compiler_params=_params(("arbitrary", "arbitrary")),
        name="modulation",
    )(c_all, w_mod, b_mod.reshape(depth, N_MOD, 1, D_MODEL))


def _rms_mod(x, g, scale, shift):
    y = x * lax.rsqrt(jnp.mean(x * x, axis=-1, keepdims=True) + RMS_EPS) * g
    return y * (1.0 + scale) + shift


def _inproj_kernel(x_ref, sc_ref, sh_ref, g_ref, w_ref, *o_refs, hp):
    nb, tt, d = x_ref.shape
    h = _rms_mod(x_ref[...], g_ref[...], sc_ref[...], sh_ref[...]).reshape(nb * tt, d)
    if not hp:
        h = h.astype(BF16)
    off = 0
    for o_ref in o_refs:
        w = o_ref.shape[-1]
        o_ref[...] = _dot(h, w_ref[:, off:off + w], hp).reshape(nb, tt, w)
        off += w


def _in_projection(x, scale, shift, g, w_in, nb, tt, hp):
    B, T, _ = x.shape
    p_total = w_in.shape[1]
    tok = lambda w: pl.BlockSpec((nb, tt, w), lambda i, j: (i, j, 0))
    per_seq = pl.BlockSpec((nb, 1, D_MODEL), lambda i, j: (i, 0, 0))
    return pl.pallas_call(
        functools.partial(_inproj_kernel, hp=hp),
        grid=(B // nb, T // tt),
        in_specs=[tok(D_MODEL), per_seq, per_seq,
                  pl.BlockSpec((1, 1, D_MODEL), lambda i, j: (0, 0, 0)),
                  pl.BlockSpec((D_MODEL, p_total), lambda i, j: (0, 0))],
        out_specs=[tok(w) for w in SEG_WIDTHS],
        out_shape=[jax.ShapeDtypeStruct((B, T, w), F32) for w in SEG_WIDTHS],
        compiler_params=_params(("arbitrary", "arbitrary")),
        name="in_projection",
    )(x, scale, shift, g.reshape(1, 1, D_MODEL), w_in)


def _rwkv_kernel(p_ref, prev0_ref, s0_ref, mu_ref, w0_ref, wup_ref, a0_ref, aup_ref, gup_ref, kk_ref, ka_ref,
                 rk_ref, lng_ref, lnb_ref, y_ref, s_out_ref, s_scr, carry_scr, *, ns, ts, hp):
    C = ns * ts
    W = RWKV_W
    chunk = pl.program_id(1)
    dot = functools.partial(_dot, hp=hp)
    dot_nt = functools.partial(_dot_nt, hp=hp)
    dot_tn = functools.partial(_dot_tn, hp=hp)

    @pl.when(chunk == 0)
    def _():
        s_scr[...] = s0_ref[...]
        carry_scr[...] = prev0_ref[...]

    p = p_ref[...].reshape(C, RWKV_PROJ)
    row1 = lax.broadcasted_iota(jnp.int32, (C, 1), 0)
    prev_first = jnp.broadcast_to(carry_scr[...], (ns, ts, RWKV_PROJ)).reshape(C, RWKV_PROJ)
    prev = jnp.where(_imod(row1, ts) == 0, prev_first, pltpu.roll(p, 1, 0))
    carry_scr[...] = p_ref[:, ts - 1:ts, :]
    ps = p + mu_ref[...] * (prev - p)

    r = ps[:, 0:W]
    k = ps[:, W:2 * W]
    v = ps[:, 2 * W:3 * W]
    lora = ps[:, 3 * W:]
    w = -_softplus(-(w0_ref[...] + dot(jnp.tanh(lora), wup_ref[...]))) - 0.5
    lw = -jnp.exp(w)
    a = _sigmoid(a0_ref[...] + dot(lora, aup_ref[...]))
    g = dot(_sigmoid(lora), gup_ref[...])

    head_bd = _head_block_mask(W)
    kkr = k * kk_ref[...]
    kk = kkr * lax.rsqrt(jnp.maximum(_dot_exact_rhs(kkr * kkr, head_bd), 1e-24))
    kh = k * (1.0 + (a - 1.0) * ka_ref[...])
    a_vec = -kk
    b_vec = kk * a

    rowi = lax.broadcasted_iota(jnp.int32, (C, C), 0)
    coli = lax.broadcasted_iota(jnp.int32, (C, C), 1)
    same = _idiv(rowi, ts) == _idiv(coli, ts)
    strict = jnp.logical_and(same, coli < rowi)
    incl = jnp.logical_and(same, coli <= rowi)
    L = _dot_exact_lhs(jnp.where(incl, 1.0, 0.0).astype(BF16), lw, 3)
    L_end = _dot_exact_lhs(jnp.where(same, 1.0, 0.0).astype(BF16), lw, 3)
    e_neg = jnp.exp(-L)
    e_end = jnp.exp(L_end - L)
    At = a_vec * jnp.exp(L - lw)
    Rt = r * jnp.exp(L)
    Bt = b_vec * e_neg
    Kt = kh * e_neg
    Bh = b_vec * e_end
    Kh = kh * e_end
    P_end = jnp.exp(L_end)

    lane_head = _idiv(lax.broadcasted_iota(jnp.int32, (1, W), 1), HEAD_DIM)
    eye =jnp.where(rowi == coli, 1.0, 0.0)
    n_factors = max(1, (ts - 1).bit_length())
    T_heads, rb_heads = [], []
    akv = jnp.zeros((C, W), F32)
    rkv = jnp.zeros((C, W), F32)
    for h in range(RWKV_HEADS):
        mh = lane_head == h
        At_h = jnp.where(mh, At, 0.0)
        Rt_h = jnp.where(mh, Rt, 0.0)
        ab = jnp.where(strict, dot_nt(At_h, Bt), 0.0)
        ak = jnp.where(strict, dot_nt(At_h, Kt), 0.0)
        rb = jnp.where(incl, dot_nt(Rt_h, Bt), 0.0)
        rk = jnp.where(incl, dot_nt(Rt_h, Kt), 0.0)
        T = eye + ab
        apow = ab
        for _ in range(n_factors - 1):
            apow = dot(apow, apow)
            T = T + dot(T, apow)
        akv = akv + jnp.where(mh, dot(ak, v), 0.0)
        rkv = rkv + jnp.where(mh, dot(rk, v), 0.0)
        T_heads.append(T)
        rb_heads.append(rb)

    xs, ys = [], []
    for q in range(ns):
        rows = slice(q * ts, (q + 1) * ts)
        Sq = s_scr[q]
        xs.append(dot_nt(At[rows], Sq))
        ys.append(dot_nt(Rt[rows], Sq))
    X = xs[0] if ns == 1 else jnp.concatenate(xs, axis=0)
    YS = ys[0] if ns == 1 else jnp.concatenate(ys, axis=0)
    rhs = X + akv
    U = jnp.zeros((C, W), F32)
    for h in range(RWKV_HEADS):
        U = U + jnp.where(lane_head == h, dot(T_heads[h], rhs), 0.0)
    Y = YS + rkv
    for h in range(RWKV_HEADS):
        Y = Y + jnp.where(lane_head == h, dot(rb_heads[h], U), 0.0)

    bd_f32 = _head_block_mask(W, F32)
    for q in range(ns):
        rows = slice(q * ts, (q + 1) * ts)
        upd = dot_tn(U[rows], Bh[rows]) + dot_tn(v[rows], Kh[rows])
        s_scr[q] = (s_scr[q] * P_end[q * ts:q * ts + 1, :] + upd) * bd_f32

    inv_n = 1.0 / HEAD_DIM
    mean = _dot_exact_rhs(Y, head_bd) * inv_n
    yc = Y - mean
    var = _dot_exact_rhs(yc * yc, head_bd) * inv_n
    yn = yc * lax.rsqrt(var + GN_EPS) * lng_ref[...] + lnb_ref[...]
    bonus = _dot_exact_rhs(r * kh * rk_ref[...], head_bd) * v
    y_ref[...] = ((yn + bonus) * g).reshape(ns, ts, W)

    @pl.when(chunk == pl.num_programs(1) - 1)
    def _():
        s_out_ref[...] = s_scr[...]


def _rwkv_mix(p_rwkv, prev0, s0_full, lw, ns, ts, hp):
    B, T, _ = p_rwkv.shape
    W = RWKV_W
    vec = lambda n: pl.BlockSpec((1, n), lambda i, j: (0, 0))
    mat = lambda: pl.BlockSpec((LORA_W, W), lambda i, j: (0, 0))
    kern = functools.partial(_rwkv_kernel, ns=ns, ts=ts, hp=hp)
    return pl.pallas_call(
        kern,
        grid=(B // ns, T // ts),
        in_specs=[
            pl.BlockSpec((ns, ts, RWKV_PROJ), lambda i, j: (i, j, 0)),
            pl.BlockSpec((ns, 1, RWKV_PROJ), lambda i, j: (i, 0, 0)),
            pl.BlockSpec((ns, W, W), lambda i, j: (i, 0, 0)),
            vec(RWKV_PROJ), vec(W), mat(), vec(W), mat(), mat(), vec(W), vec(W), vec(W), vec(W), vec(W),
        ],
        out_specs=[
            pl.BlockSpec((ns, ts, W), lambda i, j: (i, j, 0)),
            pl.BlockSpec((ns, W, W), lambda i, j: (i, 0, 0)),
        ],
        out_shape=[jax.ShapeDtypeStruct((B, T, W), F32), jax.ShapeDtypeStruct((B, W, W), F32)],
        scratch_shapes=[pltpu.VMEM((ns, W, W), F32), pltpu.VMEM((ns, 1, RWKV_PROJ), F32)],
        compiler_params=_params(("arbitrary", "arbitrary")),
        name="rwkv_mix",
    )(p_rwkv, prev0, s0_full, lw['mu'], lw['w0'], lw['w_up'], lw['a0'], lw['a_up'], lw['g_up'],
      lw['k_k'], lw['k_a'], lw['r_k'], lw['ln_g'], lw['ln_b'])


def _state_to_full(s):
    B = s.shape[0]
    eye = jnp.eye(RWKV_HEADS, dtype=s.dtype)
    return jnp.einsum('bhij,hg->bhigj', s, eye).reshape(B, RWKV_W, RWKV_W)


def _state_from_full(sf):
    B = sf.shape[0]
    s5 = sf.reshape(B, RWKV_HEADS, HEAD_DIM, RWKV_HEADS, HEAD_DIM)
    return jnp.stack([s5[:, h, :, h, :] for h in range(RWKV_HEADS)], axis=1)


def _gmlp_kernel(u_ref, v_ref, lng_ref, lnb_ref, ws_ref, bias_ref, y_ref, vrow_ref, *, hp):
    W = GMLP_W
    head_bd = _head_block_mask(W)
    inv_n = 1.0 / HEAD_DIM
    row = lax.broadcasted_iota(jnp.int32, (CHUNK, CHUNK), 0)
    col = lax.broadcasted_iota(jnp.int32, (CHUNK, CHUNK), 1)
    tril = col <= row
    lane_group = _idiv(lax.broadcasted_iota(jnp.int32, (1, W), 1), HEAD_DIM)
    wms = [jnp.where(tril, ws_ref[gi], 0.0) for gi in range(GMLP_GROUPS)]
    if not hp:
        wms = [wm.astype(BF16) for wm in wms]
    for ci in range(u_ref.shape[0]):
        u = _gelu_tanh(u_ref[ci])
        v = _gelu_tanh(v_ref[ci])
        mean = _dot_exact_rhs(v, head_bd) * inv_n
        vc = v - mean
        var = _dot_exact_rhs(vc * vc, head_bd) * inv_n
        vn = vc * lax.rsqrt(var + LN_EPS) * lng_ref[...] + lnb_ref[...]
        vrow_ref[ci] = vn
        mixed = bias_ref[...]
        for gi in range(GMLP_GROUPS):
            mixed = mixed + _dot(wms[gi], jnp.where(lane_group == gi, vn, 0.0), hp)
        y_ref[ci] = u * mixed


def _gmlp_mix(u, v, ln_g, ln_b, ws_chunk, bias_chunk, hp):
    n = u.shape[0]
    nc = GMLP_CHUNKS_PER_STEP
    blk = pl.BlockSpec((nc, CHUNK, GMLP_W), lambda i: (i, 0, 0))
    vec = pl.BlockSpec((1, GMLP_W), lambda i: (0, 0))
    return pl.pallas_call(
        functools.partial(_gmlp_kernel, hp=hp),
        grid=(n // nc,),
        in_specs=[blk, blk, vec, vec,
                  pl.BlockSpec((GMLP_GROUPS, CHUNK, CHUNK), lambda i: (0, 0, 0)),
                  pl.BlockSpec((CHUNK, GMLP_W), lambda i: (0, 0))],
        out_specs=[blk, blk],
        out_shape=[jax.ShapeDtypeStruct(u.shape, F32), jax.ShapeDtypeStruct(u.shape, F32)],
        compiler_params=_params(("arbitrary",)),
        name="gmlp_mix",
    )(u, v, ln_g, ln_b, ws_chunk, bias_chunk)


def _sb_block(qb, kb, vb, bias, U, carry, acc, mask, hp=False):
    m = qb.shape[0]
    z = _dot_nt(qb, kb, hp) + bias
    sp = jnp.maximum(z, 0.0) + jnp.log(1.0 + jnp.exp(-jnp.abs(z)))
    log_not = -sp
    if mask is not None:
        log_not = jnp.where(mask, log_not, 0.0)
    hi, lo = _split(log_not, 2)
    both = jnp.dot(jnp.concatenate([hi, lo], axis=0), U, preferred_element_type=F32)
    local = both[:m] + both[m:]
    a = jnp.exp(z - sp + (local + carry))
    if mask is not None:
        a = jnp.where(mask, a, 0.0)
    acc = acc + _dot(a, vb, hp)
    carry = carry + (local[:, 0:1] + log_not[:, 0:1])
    return carry, acc


def _suffix_matrix(n):
    r = lax.broadcasted_iota(jnp.int32, (n, n), 0)
    c = lax.broadcasted_iota(jnp.int32, (n, n), 1)
    return jnp.where(r > c, 1.0, 0.0).astype(BF16)


def _sb_prompt_kernel(bias_ref, q_ref, k_ref, v_ref, o_ref, *, blk, strip):
    h = pl.program_id(1)
    qi = pl.program_id(2)
    bias = bias_ref[h]
    n_strips = blk // strip
    qs = [(q_ref[0, 0, s * strip:(s + 1) * strip, :] * (HEAD_DIM ** -0.5)).astype(BF16) for s in range(n_strips)]
    U = _suffix_matrix(blk)
    row = lax.broadcasted_iota(jnp.int32, (strip, blk), 0)
    col = lax.broadcasted_iota(jnp.int32, (strip, blk), 1)

    def kv(j):
        start = pl.multiple_of(j * blk, blk)
        return k_ref[0, 0, pl.ds(start, blk), :].astype(BF16), v_ref[0, 0, pl.ds(start, blk), :].astype(BF16)

    def sweep(js, state, diagonal):
        units = [(t, s) for t in range(len(js)) for s in range(n_strips)]
        kvs = [kv(j) for j in js]
        masks = [(col < row + s * strip) if diagonal else None for _, s in units]
        zs = [_dot_nt(qs[s], kvs[j][0]) + bias for j, s in units]
        sps = [jnp.maximum(z, 0.0) + jnp.log(1.0 + jnp.exp(-jnp.abs(z))) for z in zs]
        lns = [-sp if m is None else jnp.where(m, -sp, 0.0) for sp, m in zip(sps, masks)]
        boths = [jnp.dot(jnp.concatenate(_split(ln, 2), axis=0), U, preferred_element_type=F32) for ln in lns]
        locs = [b[:strip] + b[strip:] for b in boths]
        state = list(state)
        for u, (j, s) in enumerate(units):
            a = jnp.exp(zs[u] - sps[u] + (locs[u] + state[2 * s]))
            if diagonal:
                a = jnp.where(masks[u], a, 0.0)
            state[2 * s] = state[2 * s] + (locs[u][:, 0:1] + lns[u][:, 0:1])
            state[2 * s + 1] = state[2 * s + 1] + _dot(a, kvs[j][1])
        return tuple(state)

    state = tuple(jnp.zeros((strip, w), F32) for _ in range(n_strips) for w in (1, HEAD_DIM))
    state = sweep([qi], state, True)
    odd = qi % 2
    state = lax.cond(odd == 1, lambda st: sweep([qi - 1], st, False), lambda st: st, state)
    top = qi - odd - 1
    state = lax.fori_loop(0, qi // 2, lambda i, st: sweep([top - 2 * i, top - 2 * i - 1], st, False), state)
    for s in range(n_strips):
        o_ref[0, 0, s * strip:(s + 1) * strip, :] = state[2 * s + 1]


def _sb_prompt(q, k, v, bias):
    B, H, T, _ = q.shape
    blk = SB_BLOCK
    qspec = pl.BlockSpec((1, 1, blk, HEAD_DIM), lambda b, h, i: (b, h, i, 0))
    kvspec = pl.BlockSpec((1, 1, T, HEAD_DIM), lambda b, h, i: (b, h, 0, 0))
    return pl.pallas_call(
        functools.partial(_sb_prompt_kernel, blk=blk, strip=SB_STRIP),
        grid=(B, H, T // blk),
        in_specs=[pl.BlockSpec(memory_space=pltpu.SMEM), qspec, kvspec, kvspec],
        out_specs=qspec,
        out_shape=jax.ShapeDtypeStruct(q.shape, F32),
        compiler_params=_params(("arbitrary", "arbitrary", "arbitrary")),
        name="sb_prompt",
    )(bias, q, k, v)


def _sb_sample_kernel(pt_ref, q_ref, kn_ref, vn_ref, bias_ref, *refs, pages_per_step):
    P = pages_per_step
    k_pages = refs[:P]
    v_pages = refs[P:2 * P]
    o_ref, carry_scr, acc_scr = refs[2 * P:]
    g = pl.program_id(1)
    T = q_ref.shape[1]
    R = SB_HEADS * T
    q = q_ref[0] * (HEAD_DIM ** -0.5)
    q_rows = jnp.concatenate([q] * SB_HEADS, axis=0)
    row_head = _idiv(lax.broadcasted_iota(jnp.int32, (R, SB_W), 0), T)
    lane_head = _idiv(lax.broadcasted_iota(jnp.int32, (R, SB_W), 1), HEAD_DIM)
    own = row_head == lane_head
    q_hi, q_lo = _split(jnp.where(own, q_rows, 0.0), 2)
    q_both = jnp.concatenate([q_hi, q_lo], axis=0)
    U = _suffix_matrix(PAGE_SIZE)
    bias = bias_ref[...]

    def load_page(ref):
        heads = [ref[0, 0, pl.ds(h, PAGE_SIZE, stride=SB_HEADS), :] for h in range(SB_HEADS)]
        return jnp.concatenate(heads, axis=-1)

    def run(blocks, carry, acc):
        zs = []
        for k, _, _ in blocks:
            k_hi, k_lo = _split(k, 2)
            d = _dot_nt(q_both, k_hi)
            zs.append(d[:R] + d[R:] + _dot_nt(q_hi, k_lo) + bias)
        sps = [jnp.maximum(z, 0.0) + jnp.log(1.0 + jnp.exp(-jnp.abs(z))) for z in zs]
        lns = [-sp if m is None else jnp.where(m, -sp, 0.0) for sp, (_, _, m) in zip(sps, blocks)]
        boths = [jnp.dot(jnp.concatenate(_split(ln, 2), axis=0), U, preferred_element_type=F32) for ln in lns]
        locs = [b[:R] + b[R:] for b in boths]
        for z, sp, ln, loc, (_, v, m) in zip(zs, sps, lns, locs, blocks):
            a = jnp.exp(z - sp + (loc + carry))
            if m is not None:
                a = jnp.where(m, a, 0.0)
            carry = carry + (loc[:, 0:1] + ln[:, 0:1])
            a_hi, a_lo = _split(a, 2)
            v_hi, v_lo = _split(v, 2)
            d = _dot(jnp.concatenate([a_hi, a_lo], axis=0), v_hi)
            acc = acc + (d[:R] + d[R:] + _dot(a_hi, v_lo))
        return carry, acc

    @pl.when(g == 0)
    def _():
        pad = jnp.zeros((PAGE_SIZE - T, SB_W), F32)
        k_new = jnp.concatenate([kn_ref[0], pad], axis=0)
        v_new = jnp.concatenate([vn_ref[0], pad], axis=0)
        tok = _imod(lax.broadcasted_iota(jnp.int32, (R, PAGE_SIZE), 0), T)
        key = lax.broadcasted_iota(jnp.int32, (R, PAGE_SIZE), 1)
        carry, acc = run([(k_new, v_new, key < tok)], jnp.zeros((R, 1), F32), jnp.zeros((R, SB_W), F32))
        carry_scr[...] = carry
        acc_scr[...] = acc

    blocks = [(load_page(k_pages[j]), load_page(v_pages[j]), None) for j in reversed(range(P))]
    carry, acc = run(blocks, carry_scr[...], acc_scr[...])
    carry_scr[...] = carry
    acc_scr[...] = acc

    @pl.when(g == pl.num_programs(1) - 1)
    def _():
        mine = jnp.where(own, acc_scr[...], 0.0)
        out = mine[0:T]
        for h in range(1, SB_HEADS):
            out = out + mine[h * T:(h + 1) * T]
        o_ref[0] = out


def _sb_sample(q, k_new, v_new, cache_k, cache_v, page_table_flat, bias_rows, layer, n_pages):
    DB, T, _ = q.shape
    P = SB_PAGES_PER_STEP
    n_steps = n_pages // P
    tok = pl.BlockSpec((1, T, SB_W), lambda b, g, pt: (b, 0, 0))

    def page_spec(j):
        return pl.BlockSpec((1, 1, PAGE_SIZE * SB_HEADS, HEAD_DIM),
                            lambda b, g, pt, j=j: (layer, pt[b * n_pages + (n_steps - 1 - g) * P + j], 0, 0))

    grid_spec = pltpu.PrefetchScalarGridSpec(
        num_scalar_prefetch=1,
        grid=(DB, n_steps),
        in_specs=[tok, tok, tok, pl.BlockSpec((SB_HEADS * T, PAGE_SIZE), lambda b, g, pt: (0, 0))]
        + [page_spec(j) for j in range(P)] * 2,
        out_specs=tok,
        scratch_shapes=[pltpu.VMEM((SB_HEADS * T, 1), F32), pltpu.VMEM((SB_HEADS * T, SB_W), F32)],
    )
    return pl.pallas_call(
        functools.partial(_sb_sample_kernel, pages_per_step=P),
        grid_spec=grid_spec,
        out_shape=jax.ShapeDtypeStruct(q.shape, F32),
        compiler_params=_params(("arbitrary", "arbitrary")),
        name="sb_sample",
    )(page_table_flat, q, k_new, v_new, bias_rows, *([cache_k] * P), *([cache_v] * P))


def _route(logits):
    lane_i = lax.broadcasted_iota(jnp.int32, logits.shape, 1)
    lane = lane_i.astype(F32)
    neg = -jnp.inf
    none = float(ROUTER_LANES)
    is_g = lane_i < N_GROUPS
    lg = jnp.where(is_g, logits, neg)
    mg = jnp.max(lg, axis=-1, keepdims=True)
    pg_top = 1.0 / jnp.sum(jnp.where(is_g, jnp.exp(lg - mg), 0.0), axis=-1, keepdims=True)
    g_idx = jnp.min(jnp.where(lg == mg, lane, none), axis=-1, keepdims=True)
    e_rel = lane_i - EXPERT_LANE0
    lane_group = _idiv(e_rel, EXPERTS_PER_GROUP).astype(F32)
    in_group = jnp.logical_and(jnp.logical_and(e_rel >= 0, e_rel < N_EXPERTS), lane_group == g_idx)
    el = jnp.where(in_group, logits, neg)
    m1 = jnp.max(el, axis=-1, keepdims=True)
    i1 = jnp.min(jnp.where(el == m1, lane, none), axis=-1, keepdims=True)
    el2 = jnp.where(lane == i1, neg, el)
    m2 = jnp.max(el2, axis=-1, keepdims=True)
    i2 = jnp.min(jnp.where(el2 == m2, lane, none), axis=-1, keepdims=True)
    t = jnp.exp(m2 - m1)
    w1 = 1.0 / (1.0 + t)
    w2 = t / (1.0 + t)
    return jnp.where(lane == i1, pg_top * w1, 0.0) + jnp.where(lane == i2, pg_top * w2, 0.0)


def _outmoe_kernel(x_ref, yr_ref, yg_ref, ys_ref, g1_ref, sc2_ref, sh2_ref, g2_ref, n2g_ref, fg_ref, wout_ref,
                   wr_ref, br_ref, wg_ref, wu_ref, wd_ref, o_ref, x1_scr, h2_scr, dw_scr, acc_scr, *, final, hp):
    nb, tt, d = x_ref.shape
    rows = nb * tt
    e = pl.program_id(2)

    @pl.when(e == 0)
    def _():
        flat = lambda ref: ref[...].reshape(rows, ref.shape[-1])
        mixed = (_dot(flat(yr_ref), wout_ref[0:RWKV_W], hp)
                 + _dot(flat(yg_ref), wout_ref[RWKV_W:RWKV_W + GMLP_W], hp)
                 + _dot(flat(ys_ref), wout_ref[RWKV_W + GMLP_W:], hp))
        x1 = x_ref[...] + g1_ref[...] * mixed.reshape(nb, tt, d)
        h2 = _rms_mod(x1, n2g_ref[...], sc2_ref[...], sh2_ref[...]).reshape(rows, d)
        x1_scr[...] = x1
        h2_scr[...] = h2.astype(BF16)
        logits = jnp.dot(h2, wr_ref[...], preferred_element_type=F32, precision=lax.Precision.HIGHEST) + br_ref[...]
        dw_scr[...] = _route(logits)
        acc_scr[...] = jnp.zeros_like(acc_scr)

    hb = h2_scr[...]
    gate = jnp.dot(hb, wg_ref[0, 0], preferred_element_type=F32)
    up = jnp.dot(hb, wu_ref[0, 0], preferred_element_type=F32)
    lane = lax.broadcasted_iota(jnp.int32, (rows, ROUTER_LANES), 1)
    w_e = jnp.sum(jnp.where(lane == e + EXPERT_LANE0, dw_scr[...], 0.0), axis=-1, keepdims=True)
    hid = gate * _sigmoid(gate) * up * w_e
    acc_scr[...] += jnp.dot(hid.astype(BF16), wd_ref[0, 0], preferred_element_type=F32)

    @pl.when(e == pl.num_programs(2) - 1)
    def _():
        x2 = x1_scr[...] + g2_ref[...] * acc_scr[...].reshape(nb, tt, d)
        if final:
            x2 = x2 * lax.rsqrt(jnp.mean(x2 * x2, axis=-1, keepdims=True) + RMS_EPS) * fg_ref[...]
        o_ref[...] = x2


def _out_moe(x, y_rwkv, y_gmlp, y_sb, gate1, scale2, shift2, gate2, lw, layer, final_g, final, nb, tt, hp):
    B, T, _ = x.shape
    rows = nb * tt
    tok = lambda w: pl.BlockSpec((nb, tt, w), lambda i, j, e: (i, j, 0))
    per_seq = pl.BlockSpec((nb, 1, D_MODEL), lambda i, j, e: (i, 0, 0))
    vec = pl.BlockSpec((1, 1, D_MODEL), lambda i, j, e: (0, 0, 0))
    return pl.pallas_call(
        functools.partial(_outmoe_kernel, final=final, hp=hp),
        grid=(B // nb, T // tt, N_EXPERTS),
        in_specs=[tok(D_MODEL), tok(RWKV_W), tok(GMLP_W), tok(SB_W), per_seq, per_seq, per_seq, per_seq, vec, vec,
                  pl.BlockSpec((D_MODEL, D_MODEL), lambda i, j, e: (0, 0)),
                  pl.BlockSpec((D_MODEL, ROUTER_LANES), lambda i, j, e: (0, 0)),
                  pl.BlockSpec((1, ROUTER_LANES), lambda i, j, e: (0, 0)),
                  pl.BlockSpec((1, 1, D_MODEL, D_EXPERT), lambda i, j, e: (layer, e, 0, 0)),
                  pl.BlockSpec((1, 1, D_MODEL, D_EXPERT), lambda i, j, e: (layer, e, 0, 0)),
                  pl.BlockSpec((1, 1, D_EXPERT, D_MODEL), lambda i, j, e: (layer, e, 0, 0))],
        out_specs=tok(D_MODEL),
        out_shape=jax.ShapeDtypeStruct(x.shape, F32),
        scratch_shapes=[pltpu.VMEM((nb, tt, D_MODEL), F32), pltpu.VMEM((rows, D_MODEL), BF16),
                        pltpu.VMEM((rows, ROUTER_LANES), F32), pltpu.VMEM((rows, D_MODEL), F32)],
        compiler_params=_params(("arbitrary", "arbitrary", "arbitrary")),
        name="out_moe",
    )(x, y_rwkv, y_gmlp, y_sb, gate1, scale2, shift2, gate2, lw['norm2_g'], final_g,
      lw['w_out_f32'] if hp else lw['w_out'], lw['w_router'], lw['b_router'], lw['exp_wg'], lw['exp_wu'], lw['exp_wd'])


def _trunk_layer(x, mod, lw, layer, s0_full, prev0, gm_ws, gm_bias, sb_fn, final_g, final, nb_tok_in, tt_in,
                 nb_tok_out, tt_out, rwkv_ns, rwkv_ts, hp):
    B, T, _ = x.shape
    shift1, scale1, gate1, shift2, scale2, gate2 = mod
    p_rwkv, g_u, g_v, q, k, v = _in_projection(x, scale1, shift1, lw['norm1_g'],
                                               lw['w_in_f32'] if hp else lw['w_in'], nb_tok_in, tt_in, hp)
    y_rwkv, s_new = _rwkv_mix(p_rwkv, prev0, s0_full, lw['rwkv'], rwkv_ns, rwkv_ts, hp)
    n_chunks = B * T // CHUNK
    y_gmlp, v_rows = _gmlp_mix(g_u.reshape(n_chunks, CHUNK, GMLP_W), g_v.reshape(n_chunks, CHUNK, GMLP_W),
                               lw['gmlp_ln_g'], lw['gmlp_ln_b'], gm_ws, gm_bias, hp)
    y_gmlp = y_gmlp.reshape(B, T, GMLP_W)
    v_rows = v_rows.reshape(B, T, GMLP_W)
    y_sb = sb_fn(q, k, v)
    x_out = _out_moe(x, y_rwkv, y_gmlp, y_sb, gate1, scale2, shift2, gate2, lw, layer, final_g, final,
                     nb_tok_out, tt_out, hp)
    return x_out, s_new, p_rwkv, k, v, v_rows


def _pad_rows(w, row0):
    return jnp.zeros((LORA_W, RWKV_W), F32).at[row0:row0 + w.shape[0]].set(w)


def kernel(x_prompt, x_sample, cache_k, cache_v, page_table, state_rwkv, state_rwkv_shift, c_prompt, c_sample, norm1_g, norm2_g, w_mod, b_mod, w_in, rwkv_mu, rwkv_w0, rwkv_w_up, rwkv_a0, rwkv_a_up, rwkv_g_up, rwkv_k_k, rwkv_k_a, rwkv_r_k, rwkv_ln_g, rwkv_ln_b, gmlp_ln_g, gmlp_ln_b, gmlp_ws, gmlp_bs, sb_bias, w_out, router_g_w, router_g_b, router_e_w, router_e_b, exp_wg, exp_wu, exp_wd, final_g):
    B, T, _ = x_prompt.shape
    DB, TS, _ = x_sample.shape
    depth = w_in.shape[0]
    n_pages = page_table.shape[1]
    n_pool = cache_k.shape[1]

    mod_all = _modulation(jnp.concatenate([c_prompt, c_sample], axis=0), w_mod, b_mod)
    cache_k4 = cache_k.reshape(depth, n_pool, PAGE_SIZE * SB_HEADS, HEAD_DIM)
    cache_v4 = cache_v.reshape(depth, n_pool, PAGE_SIZE * SB_HEADS, HEAD_DIM)
    pt_flat = page_table.reshape(-1)
    exp_wg_b = exp_wg.astype(BF16)
    exp_wu_b = exp_wu.astype(BF16)
    exp_wd_b = exp_wd.astype(BF16)
    fg = final_g.reshape(1, 1, D_MODEL)
    seqs_per_chunk = CHUNK // TS

    xp, xs = x_prompt, x_sample
    outs = {n: [] for n in ('kp', 'vp', 'ks', 'vs', 'Sp', 'shp', 'Ss', 'shs', 'gv')}
    for l in range(depth):
        w_router = jnp.zeros((D_MODEL, ROUTER_LANES), F32)
        w_router = w_router.at[:, :N_GROUPS].set(router_g_w[l]).at[:, EXPERT_LANE0:EXPERT_LANE0 + N_EXPERTS].set(router_e_w[l])
        b_router = jnp.zeros((1, ROUTER_LANES), F32)
        b_router = b_router.at[0, :N_GROUPS].set(router_g_b[l]).at[0, EXPERT_LANE0:EXPERT_LANE0 + N_EXPERTS].set(router_e_b[l])
        row = lambda a: a.reshape(1, -1)
        lw = dict(
            norm1_g=norm1_g[l], norm2_g=norm2_g[l].reshape(1, 1, D_MODEL), w_in=w_in[l].astype(BF16),
            w_in_f32=w_in[l], w_out=w_out[l].astype(BF16), w_out_f32=w_out[l],
            w_router=w_router, b_router=b_router,
            exp_wg=exp_wg_b, exp_wu=exp_wu_b, exp_wd=exp_wd_b,
            gmlp_ln_g=row(gmlp_ln_g[l]), gmlp_ln_b=row(gmlp_ln_b[l]),
            rwkv=dict(mu=row(rwkv_mu[l]), w0=row(rwkv_w0[l]), w_up=_pad_rows(rwkv_w_up[l], 0), a0=row(rwkv_a0[l]),
                      a_up=_pad_rows(rwkv_a_up[l], W_LORA), g_up=_pad_rows(rwkv_g_up[l], W_LORA + A_LORA),
                      k_k=row(rwkv_k_k[l]), k_a=row(rwkv_k_a[l]), r_k=row(rwkv_r_k[l]), ln_g=row(rwkv_ln_g[l]),
                      ln_b=row(rwkv_ln_b[l])),
        )
        final = l == depth - 1
        mods_p = [mod_all[l, i, :B].reshape(B, 1, D_MODEL) for i in range(N_MOD)]
        mods_s = [mod_all[l, i, B:].reshape(DB, 1, D_MODEL) for i in range(N_MOD)]

        bias_p = jnp.repeat(jnp.swapaxes(gmlp_bs[l], 0, 1), HEAD_DIM, axis=1)

        def sb_prompt(q, k, v, l=l):
            hm = lambda a: jnp.transpose(a.reshape(B, T, SB_HEADS, HEAD_DIM), (0, 2, 1, 3))
            o = _sb_prompt(hm(q), hm(k), hm(v), sb_bias[l])
            return jnp.transpose(o, (0, 2, 1, 3)).reshape(B, T, SB_W)

        xp, Sp, pp, kp, vp, _ = _trunk_layer(
            xp, mods_p, lw, l, jnp.zeros((B, RWKV_W, RWKV_W), F32), jnp.zeros((B, 1, RWKV_PROJ), F32),
            gmlp_ws[l], bias_p, sb_prompt, fg, final, 1, TOK_TILE_IN, 1, TOK_TILE_OUT, RWKV_SEQS_PER_STEP,
            RWKV_CHUNK, False)

        ws_s = jnp.einsum('ab,gts->gatbs', jnp.eye(seqs_per_chunk, dtype=F32), gmlp_ws[l][:, :TS, :TS])
        ws_s = ws_s.reshape(GMLP_GROUPS, CHUNK, CHUNK)
        bias_s = jnp.tile(bias_p[:TS], (seqs_per_chunk, 1))
        bias_rows = jnp.broadcast_to(jnp.repeat(sb_bias[l], TS)[:, None], (SB_HEADS * TS, PAGE_SIZE))

        def sb_sample(q, k, v, l=l, bias_rows=bias_rows):
            return _sb_sample(q, k, v, cache_k4, cache_v4, pt_flat, bias_rows, l, n_pages)

        xs, Ss, ps_, ksm, vsm, gv = _trunk_layer(
            xs, mods_s, lw, l, _state_to_full(state_rwkv[l]), state_rwkv_shift[l].reshape(DB, 1, RWKV_PROJ),
            ws_s, bias_s, sb_sample, fg, final, TOK_TILE_SAMPLE // TS, TS, TOK_TILE_OUT // TS, TS,
            RWKV_CHUNK // TS, TS, True)

        outs['kp'].append(kp.reshape(B, T, SB_HEADS, HEAD_DIM))
        outs['vp'].append(vp.reshape(B, T, SB_HEADS, HEAD_DIM))
        outs['ks'].append(ksm.reshape(DB, TS, SB_HEADS, HEAD_DIM))
        outs['vs'].append(vsm.reshape(DB, TS, SB_HEADS, HEAD_DIM))
        outs['Sp'].append(_state_from_full(Sp))
        outs['shp'].append(pp[:, -1])
        outs['Ss'].append(_state_from_full(Ss))
        outs['shs'].append(ps_[:, -1])
        outs['gv'].append(gv)

    st = lambda n: jnp.stack(outs[n])
    return (xp, xs, st('kp'), st('vp'), st('ks'), st('vs'), st('Sp'), st('shp'), st('Ss'), st('shs'), st('gv'))
```

```python
import functools

import jax
import jax.numpy as jnp
from jax import lax
from jax.experimental import pallas as pl
from jax.experimental.pallas import tpu as pltpu

F32 = jnp.float32
BF16 = jnp.bfloat16

D_MODEL = 1024
HEAD_DIM = 64
RWKV_HEADS = 4
RWKV_W = RWKV_HEADS * HEAD_DIM
LORA_W = 128
RWKV_PROJ = 3 * RWKV_W + LORA_W
W_LORA, A_LORA, G_LORA = 32, 32, 64
GN_EPS = 64e-5
GMLP_GROUPS = 4
GMLP_W = GMLP_GROUPS * HEAD_DIM
CHUNK = 128
LN_EPS = 1e-5
SB_HEADS = 8
SB_W = SB_HEADS * HEAD_DIM
PAGE_SIZE = 128
N_GROUPS = 4
EXPERTS_PER_GROUP = 4
N_EXPERTS = N_GROUPS * EXPERTS_PER_GROUP
D_EXPERT = 256
N_MOD = 6
RMS_EPS = 1e-6
SEG_WIDTHS = (RWKV_PROJ, GMLP_W, GMLP_W, SB_W, SB_W, SB_W)

LANES = 128
ROUTER_LANES = LANES
EXPERT_LANE0 = N_GROUPS
VMEM_LIMIT = 56 * 1024 * 1024

RWKV_CHUNK = 64
SB_BLOCK = 256
SB_STRIP = 64
RWKV_SEQS_PER_STEP = 4
SB_PAGES_PER_STEP = 16
TOK_TILE_IN = 512
TOK_TILE_SAMPLE = 256
TOK_TILE_OUT = 512
GMLP_CHUNKS_PER_STEP = 4


def _params(sem):
    return pltpu.CompilerParams(dimension_semantics=sem, vmem_limit_bytes=VMEM_LIMIT)


def _mm(a, b, dims, hp):
    dg = lambda x, y: lax.dot_general(x, y, (dims, ((), ())), preferred_element_type=F32)
    if not hp:
        return dg(a.astype(BF16), b.astype(BF16))
    a_hi, a_lo = _split(a.astype(F32), 2)
    b_hi, b_lo = _split(b.astype(F32), 2)
    return dg(a_hi, b_hi) + dg(a_hi, b_lo) + dg(a_lo, b_hi)


def _dot(a, b, hp=False):
    return _mm(a, b, ((1,), (0,)), hp)


def _dot_nt(a, b, hp=False):
    return _mm(a, b, ((1,), (1,)), hp)


def _dot_tn(a, b, hp=False):
    return _mm(a, b, ((0,), (0,)), hp)


def _split(x, n):
    terms = []
    rem = x
    for _ in range(n):
        t = rem.astype(BF16)
        terms.append(t)
        rem = rem - t.astype(F32)
    return terms


def _dot_exact_rhs(x, m, n=2):
    out = None
    for t in _split(x, n):
        d = jnp.dot(t, m, preferred_element_type=F32)
        out = d if out is None else out + d
    return out


def _dot_exact_lhs(m, x, n=2):
    out = None
    for t in _split(x, n):
        d = jnp.dot(m, t, preferred_element_type=F32)
        out = d if out is None else out + d
    return out


def _softplus(z):
    return jnp.maximum(z, 0.0) + jnp.log1p(jnp.exp(-jnp.abs(z)))


def _sigmoid(z):
    return 1.0 / (1.0 + jnp.exp(-z))


def _gelu_tanh(x):
    return 0.5 * x * (1.0 + jnp.tanh(0.7978845608028654 * (x + 0.044715 * (x * x * x))))


def _idiv(x, n):
    assert n & (n - 1) == 0
    return jnp.right_shift(x, n.bit_length() - 1)


def _imod(x, n):
    assert n & (n - 1) == 0
    return jnp.bitwise_and(x, n - 1)


def _head_block_mask(n, dtype=BF16):
    r = _idiv(lax.broadcasted_iota(jnp.int32, (n, n), 0), HEAD_DIM)
    c = _idiv(lax.broadcasted_iota(jnp.int32, (n, n), 1), HEAD_DIM)
    return jnp.where(r == c, 1.0, 0.0).astype(dtype)


def _mod_kernel(c_ref, w_ref, b_ref, o_ref):
    c = c_ref[...]
    o_ref[0, 0] = _dot(c * _sigmoid(c), w_ref[0], hp=True) + b_ref[0, 0]


def _modulation(c_all, w_mod, b_mod):
    depth = w_mod.shape[0]
    rows = c_all.shape[0]
    return pl.pallas_call(
        _mod_kernel,
        grid=(depth, N_MOD),
        in_specs=[
            pl.BlockSpec((rows, D_MODEL), lambda l, j: (0, 0)),
            pl.BlockSpec((1, D_MODEL, D_MODEL), lambda l, j: (l, 0, j)),
            pl.BlockSpec((1, 1, 1, D_MODEL), lambda l, j: (l, j, 0, 0)),
        ],
        out_specs=pl.BlockSpec((1, 1, rows, D_MODEL), lambda l, j: (l, j, 0, 0)),
        out_shape=jax.ShapeDtypeStruct((depth, N_MOD, rows, D_MODEL), F32),
        compiler_params=_params(("arbitrary", "arbitrary")),
        name="modulation",
    )(c_all, w_mod, b_mod.reshape(depth, N_MOD, 1, D_MODEL))


def _rms_mod(x, g, scale, shift):
    y = x * lax.rsqrt(jnp.mean(x * x, axis=-1, keepdims=True) + RMS_EPS) * g
    return y * (1.0 + scale) + shift


def _inproj_kernel(x_ref, sc_ref, sh_ref, g_ref, w_ref, *o_refs, hp):
    nb, tt, d = x_ref.shape
    h = _rms_mod(x_ref[...], g_ref[...], sc_ref[...], sh_ref[...]).reshape(nb * tt, d)
    if not hp:
        h = h.astype(BF16)
    off = 0
    for o_ref in o_refs:
        w = o_ref.shape[-1]
        o_ref[...] = _dot(h, w_ref[:, off:off + w], hp).reshape(nb, tt, w)
        off += w


def _in_projection(x, scale, shift, g, w_in, nb, tt, hp):
    B, T, _ = x.shape
    p_total = w_in.shape[1]
    tok = lambda w: pl.BlockSpec((nb, tt, w), lambda i, j: (i, j, 0))
    per_seq = pl.BlockSpec((nb, 1, D_MODEL), lambda i, j: (i, 0, 0))
    return pl.pallas_call(
        functools.partial(_inproj_kernel, hp=hp),
        grid=(B // nb, T // tt),
        in_specs=[tok(D_MODEL), per_seq, per_seq,
                  pl.BlockSpec((1, 1, D_MODEL), lambda i, j: (0, 0, 0)),
                  pl.BlockSpec((D_MODEL, p_total), lambda i, j: (0, 0))],
        out_specs=[tok(w) for w in SEG_WIDTHS],
        out_shape=[jax.ShapeDtypeStruct((B, T, w), F32) for w in SEG_WIDTHS],
        compiler_params=_params(("arbitrary", "arbitrary")),
        name="in_projection",
    )(x, scale, shift, g.reshape(1, 1, D_MODEL), w_in)


def _rwkv_kernel(p_ref, prev0_ref, s0_ref, mu_ref, w0_ref, wup_ref, a0_ref, aup_ref, gup_ref, kk_ref, ka_ref,
                 rk_ref, lng_ref, lnb_ref, y_ref, s_out_ref, s_scr, carry_scr, *, ns, ts, hp):
    C = ns * ts
    W = RWKV_W
    chunk = pl.program_id(1)
    dot = functools.partial(_dot, hp=hp)
    dot_nt = functools.partial(_dot_nt, hp=hp)
    dot_tn = functools.partial(_dot_tn, hp=hp)

    @pl.when(chunk == 0)
    def _():
        s_scr[...] = s0_ref[...]
        carry_scr[...] = prev0_ref[...]

    p = p_ref[...].reshape(C, RWKV_PROJ)
    row1 = lax.broadcasted_iota(jnp.int32, (C, 1), 0)
    prev_first = jnp.broadcast_to(carry_scr[...], (ns, ts, RWKV_PROJ)).reshape(C, RWKV_PROJ)
    prev = jnp.where(_imod(row1, ts) == 0, prev_first, pltpu.roll(p, 1, 0))
    carry_scr[...] = p_ref[:, ts - 1:ts, :]
    ps = p + mu_ref[...] * (prev - p)

    r = ps[:, 0:W]
    k = ps[:, W:2 * W]
    v = ps[:, 2 * W:3 * W]
    lora = ps[:, 3 * W:]
    w = -_softplus(-(w0_ref[...] + dot(jnp.tanh(lora), wup_ref[...]))) - 0.5
    lw = -jnp.exp(w)
    a = _sigmoid(a0_ref[...] + dot(lora, aup_ref[...]))
    g = dot(_sigmoid(lora), gup_ref[...])

    head_bd = _head_block_mask(W)
    kkr = k * kk_ref[...]
    kk = kkr * lax.rsqrt(jnp.maximum(_dot_exact_rhs(kkr * kkr, head_bd), 1e-24))
    kh = k * (1.0 + (a - 1.0) * ka_ref[...])
    a_vec = -kk
    b_vec = kk * a

    rowi = lax.broadcasted_iota(jnp.int32, (C, C), 0)
    coli = lax.broadcasted_iota(jnp.int32, (C, C), 1)
    same = _idiv(rowi, ts) == _idiv(coli, ts)
    strict = jnp.logical_and(same, coli < rowi)
    incl = jnp.logical_and(same, coli <= rowi)
    L = _dot_exact_lhs(jnp.where(incl, 1.0, 0.0).astype(BF16), lw, 3)
    L_end = _dot_exact_lhs(jnp.where(same, 1.0, 0.0).astype(BF16), lw, 3)
    e_neg = jnp.exp(-L)
    e_end = jnp.exp(L_end - L)
    At = a_vec * jnp.exp(L - lw)
    Rt = r * jnp.exp(L)
    Bt = b_vec * e_neg
    Kt = kh * e_neg
    Bh = b_vec * e_end
    Kh = kh * e_end
    P_end = jnp.exp(L_end)

    lane_head = _idiv(lax.broadcasted_iota(jnp.int32, (1, W), 1), HEAD_DIM)
    eye =jnp.where(rowi == coli, 1.0, 0.0)
    n_factors = max(1, (ts - 1).bit_length())
    T_heads, rb_heads = [], []
    akv = jnp.zeros((C, W), F32)
    rkv = jnp.zeros((C, W), F32)
    for h in range(RWKV_HEADS):
        mh = lane_head == h
        At_h = jnp.where(mh, At, 0.0)
        Rt_h = jnp.where(mh, Rt, 0.0)
        ab = jnp.where(strict, dot_nt(At_h, Bt), 0.0)
        ak = jnp.where(strict, dot_nt(At_h, Kt), 0.0)
        rb = jnp.where(incl, dot_nt(Rt_h, Bt), 0.0)
        rk = jnp.where(incl, dot_nt(Rt_h, Kt), 0.0)
        T = eye + ab
        apow = ab
        for _ in range(n_factors - 1):
            apow = dot(apow, apow)
            T = T + dot(T, apow)
        akv = akv + jnp.where(mh, dot(ak, v), 0.0)
        rkv = rkv + jnp.where(mh, dot(rk, v), 0.0)
        T_heads.append(T)
        rb_heads.append(rb)

    xs, ys = [], []
    for q in range(ns):
        rows = slice(q * ts, (q + 1) * ts)
        Sq = s_scr[q]
        xs.append(dot_nt(At[rows], Sq))
        ys.append(dot_nt(Rt[rows], Sq))
    X = xs[0] if ns == 1 else jnp.concatenate(xs, axis=0)
    YS = ys[0] if ns == 1 else jnp.concatenate(ys, axis=0)
    rhs = X + akv
    U = jnp.zeros((C, W), F32)
    for h in range(RWKV_HEADS):
        U = U + jnp.where(lane_head == h, dot(T_heads[h], rhs), 0.0)
    Y = YS + rkv
    for h in range(RWKV_HEADS):
        Y = Y + jnp.where(lane_head == h, dot(rb_heads[h], U), 0.0)

    bd_f32 = _head_block_mask(W, F32)
    for q in range(ns):
        rows = slice(q * ts, (q + 1) * ts)
        upd = dot_tn(U[rows], Bh[rows]) + dot_tn(v[rows], Kh[rows])
        s_scr[q] = (s_scr[q] * P_end[q * ts:q * ts + 1, :] + upd) * bd_f32

    inv_n = 1.0 / HEAD_DIM
    mean = _dot_exact_rhs(Y, head_bd) * inv_n
    yc = Y - mean
    var = _dot_exact_rhs(yc * yc, head_bd) * inv_n
    yn = yc * lax.rsqrt(var + GN_EPS) * lng_ref[...] + lnb_ref[...]
    bonus = _dot_exact_rhs(r * kh * rk_ref[...], head_bd) * v
    y_ref[...] = ((yn + bonus) * g).reshape(ns, ts, W)

    @pl.when(chunk == pl.num_programs(1) - 1)
    def _():
        s_out_ref[...] = s_scr[...]


def _rwkv_mix(p_rwkv, prev0, s0_full, lw, ns, ts, hp):
    B, T, _ = p_rwkv.shape
    W = RWKV_W
    vec = lambda n: pl.BlockSpec((1, n), lambda i, j: (0, 0))
    mat = lambda: pl.BlockSpec((LORA_W, W), lambda i, j: (0, 0))
    kern = functools.partial(_rwkv_kernel, ns=ns, ts=ts, hp=hp)
    return pl.pallas_call(
        kern,
        grid=(B // ns, T // ts),
        in_specs=[
            pl.BlockSpec((ns, ts, RWKV_PROJ), lambda i, j: (i, j, 0)),
            pl.BlockSpec((ns, 1, RWKV_PROJ), lambda i, j: (i, 0, 0)),
            pl.BlockSpec((ns, W, W), lambda i, j: (i, 0, 0)),
            vec(RWKV_PROJ), vec(W), mat(), vec(W), mat(), mat(), vec(W), vec(W), vec(W), vec(W), vec(W),
        ],
        out_specs=[
            pl.BlockSpec((ns, ts, W), lambda i, j: (i, j, 0)),
            pl.BlockSpec((ns, W, W), lambda i, j: (i, 0, 0)),
        ],
        out_shape=[jax.ShapeDtypeStruct((B, T, W), F32), jax.ShapeDtypeStruct((B, W, W), F32)],
        scratch_shapes=[pltpu.VMEM((ns, W, W), F32), pltpu.VMEM((ns, 1, RWKV_PROJ), F32)],
        compiler_params=_params(("arbitrary", "arbitrary")),
        name="rwkv_mix",
    )(p_rwkv, prev0, s0_full, lw['mu'], lw['w0'], lw['w_up'], lw['a0'], lw['a_up'], lw['g_up'],
      lw['k_k'], lw['k_a'], lw['r_k'], lw['ln_g'], lw['ln_b'])


def _state_to_full(s):
    B = s.shape[0]
    eye = jnp.eye(RWKV_HEADS, dtype=s.dtype)
    return jnp.einsum('bhij,hg->bhigj', s, eye).reshape(B, RWKV_W, RWKV_W)


def _state_from_full(sf):
    B = sf.shape[0]
    s5 = sf.reshape(B, RWKV_HEADS, HEAD_DIM, RWKV_HEADS, HEAD_DIM)
    return jnp.stack([s5[:, h, :, h, :] for h in range(RWKV_HEADS)], axis=1)


def _gmlp_kernel(u_ref, v_ref, lng_ref, lnb_ref, ws_ref, bias_ref, y_ref, vrow_ref, *, hp):
    W = GMLP_W
    head_bd = _head_block_mask(W)
    inv_n = 1.0 / HEAD_DIM
    row = lax.broadcasted_iota(jnp.int32, (CHUNK, CHUNK), 0)
    col = lax.broadcasted_iota(jnp.int32, (CHUNK, CHUNK), 1)
    tril = col <= row
    lane_group = _idiv(lax.broadcasted_iota(jnp.int32, (1, W), 1), HEAD_DIM)
    wms = [jnp.where(tril, ws_ref[gi], 0.0) for gi in range(GMLP_GROUPS)]
    if not hp:
        wms = [wm.astype(BF16) for wm in wms]
    for ci in range(u_ref.shape[0]):
        u = _gelu_tanh(u_ref[ci])
        v = _gelu_tanh(v_ref[ci])
        mean = _dot_exact_rhs(v, head_bd) * inv_n
        vc = v - mean
        var = _dot_exact_rhs(vc * vc, head_bd) * inv_n
        vn = vc * lax.rsqrt(var + LN_EPS) * lng_ref[...] + lnb_ref[...]
        vrow_ref[ci] = vn
        mixed = bias_ref[...]
        for gi in range(GMLP_GROUPS):
            mixed = mixed + _dot(wms[gi], jnp.where(lane_group == gi, vn, 0.0), hp)
        y_ref[ci] = u * mixed


def _gmlp_mix(u, v, ln_g, ln_b, ws_chunk, bias_chunk, hp):
    n = u.shape[0]
    nc = GMLP_CHUNKS_PER_STEP
    blk = pl.BlockSpec((nc, CHUNK, GMLP_W), lambda i: (i, 0, 0))
    vec = pl.BlockSpec((1, GMLP_W), lambda i: (0, 0))
    return pl.pallas_call(
        functools.partial(_gmlp_kernel, hp=hp),
        grid=(n // nc,),
        in_specs=[blk, blk, vec, vec,
                  pl.BlockSpec((GMLP_GROUPS, CHUNK, CHUNK), lambda i: (0, 0, 0)),
                  pl.BlockSpec((CHUNK, GMLP_W), lambda i: (0, 0))],
        out_specs=[blk, blk],
        out_shape=[jax.ShapeDtypeStruct(u.shape, F32), jax.ShapeDtypeStruct(u.shape, F32)],
        compiler_params=_params(("arbitrary",)),
        name="gmlp_mix",
    )(u, v, ln_g, ln_b, ws_chunk, bias_chunk)


def _sb_block(qb, kb, vb, bias, U, carry, acc, mask, hp=False):
    m = qb.shape[0]
    z = _dot_nt(qb, kb, hp) + bias
    sp = jnp.maximum(z, 0.0) + jnp.log(1.0 + jnp.exp(-jnp.abs(z)))
    log_not = -sp
    if mask is not None:
        log_not = jnp.where(mask, log_not, 0.0)
    hi, lo = _split(log_not, 2)
    both = jnp.dot(jnp.concatenate([hi, lo], axis=0), U, preferred_element_type=F32)
    local = both[:m] + both[m:]
    a = jnp.exp(z - sp + (local + carry))
    if mask is not None:
        a = jnp.where(mask, a, 0.0)
    acc = acc + _dot(a, vb, hp)
    carry = carry + (local[:, 0:1] + log_not[:, 0:1])
    return carry, acc


def _suffix_matrix(n):
    r = lax.broadcasted_iota(jnp.int32, (n, n), 0)
    c = lax.broadcasted_iota(jnp.int32, (n, n), 1)
    return jnp.where(r > c, 1.0, 0.0).astype(BF16)


def _sb_prompt_kernel(bias_ref, q_ref, k_ref, v_ref, o_ref, *, blk, strip):
    h = pl.program_id(1)
    qi = pl.program_id(2)
    bias = bias_ref[h]
    n_strips = blk // strip
    qs = [(q_ref[0, 0, s * strip:(s + 1) * strip, :] * (HEAD_DIM ** -0.5)).astype(BF16) for s in range(n_strips)]
    U = _suffix_matrix(blk)
    row = lax.broadcasted_iota(jnp.int32, (strip, blk), 0)
    col = lax.broadcasted_iota(jnp.int32, (strip, blk), 1)

    def kv(j):
        start = pl.multiple_of(j * blk, blk)
        return k_ref[0, 0, pl.ds(start, blk), :].astype(BF16), v_ref[0, 0, pl.ds(start, blk), :].astype(BF16)

    def sweep(js, state, diagonal):
        units = [(t, s) for t in range(len(js)) for s in range(n_strips)]
        kvs = [kv(j) for j in js]
        masks = [(col < row + s * strip) if diagonal else None for _, s in units]
        zs = [_dot_nt(qs[s], kvs[j][0]) + bias for j, s in units]
        sps = [jnp.maximum(z, 0.0) + jnp.log(1.0 + jnp.exp(-jnp.abs(z))) for z in zs]
        lns = [-sp if m is None else jnp.where(m, -sp, 0.0) for sp, m in zip(sps, masks)]
        boths = [jnp.dot(jnp.concatenate(_split(ln, 2), axis=0), U, preferred_element_type=F32) for ln in lns]
        locs = [b[:strip] + b[strip:] for b in boths]
        state = list(state)
        for u, (j, s) in enumerate(units):
            a = jnp.exp(zs[u] - sps[u] + (locs[u] + state[2 * s]))
            if diagonal:
                a = jnp.where(masks[u], a, 0.0)
            state[2 * s] = state[2 * s] + (locs[u][:, 0:1] + lns[u][:, 0:1])
            state[2 * s + 1] = state[2 * s + 1] + _dot(a, kvs[j][1])
        return tuple(state)

    state = tuple(jnp.zeros((strip, w), F32) for _ in range(n_strips) for w in (1, HEAD_DIM))
    state = sweep([qi], state, True)
    odd = qi % 2
    state = lax.cond(odd == 1, lambda st: sweep([qi - 1], st, False), lambda st: st, state)
    top = qi - odd - 1
    state = lax.fori_loop(0, qi // 2, lambda i, st: sweep([top - 2 * i, top - 2 * i - 1], st, False), state)
    for s in range(n_strips):
        o_ref[0, 0, s * strip:(s + 1) * strip, :] = state[2 * s + 1]


def _sb_prompt(q, k, v, bias):
    B, H, T, _ = q.shape
    blk = SB_BLOCK
    qspec = pl.BlockSpec((1, 1, blk, HEAD_DIM), lambda b, h, i: (b, h, i, 0))
    kvspec = pl.BlockSpec((1, 1, T, HEAD_DIM), lambda b, h, i: (b, h, 0, 0))
    return pl.pallas_call(
        functools.partial(_sb_prompt_kernel, blk=blk, strip=SB_STRIP),
        grid=(B, H, T // blk),
        in_specs=[pl.BlockSpec(memory_space=pltpu.SMEM), qspec, kvspec, kvspec],
        out_specs=qspec,
        out_shape=jax.ShapeDtypeStruct(q.shape, F32),
        compiler_params=_params(("arbitrary", "arbitrary", "arbitrary")),
        name="sb_prompt",
    )(bias, q, k, v)


def _sb_sample_kernel(pt_ref, q_ref, kn_ref, vn_ref, bias_ref, *refs, pages_per_step):
    P = pages_per_step
    k_pages = refs[:P]
    v_pages = refs[P:2 * P]
    o_ref, carry_scr, acc_scr = refs[2 * P:]
    g = pl.program_id(1)
    T = q_ref.shape[1]
    R = SB_HEADS * T
    q = q_ref[0] * (HEAD_DIM ** -0.5)
    q_rows = jnp.concatenate([q] * SB_HEADS, axis=0)
    row_head = _idiv(lax.broadcasted_iota(jnp.int32, (R, SB_W), 0), T)
    lane_head = _idiv(lax.broadcasted_iota(jnp.int32, (R, SB_W), 1), HEAD_DIM)
    own = row_head == lane_head
    q_hi, q_lo = _split(jnp.where(own, q_rows, 0.0), 2)
    q_both = jnp.concatenate([q_hi, q_lo], axis=0)
    U = _suffix_matrix(PAGE_SIZE)
    bias = bias_ref[...]

    def run(blocks, carry, acc, transposed):
        qk = _dot if transposed else _dot_nt
        av = _dot_nt if transposed else _dot
        zs = []
        for k, _, _ in blocks:
            k_hi, k_lo = _split(k, 2)
            d = qk(q_both, k_hi)
            zs.append(d[:R] + d[R:] + qk(q_hi, k_lo) + bias)
        sps = [jnp.maximum(z, 0.0) + jnp.log(1.0 + jnp.exp(-jnp.abs(z))) for z in zs]
        lns = [-sp if m is None else jnp.where(m, -sp, 0.0) for sp, (_, _, m) in zip(sps, blocks)]
        boths = [jnp.dot(jnp.concatenate(_split(ln, 2), axis=0), U, preferred_element_type=F32) for ln in lns]
        locs = [b[:R] + b[R:] for b in boths]
        for z, sp, ln, loc, (_, v, m) in zip(zs, sps, lns, locs, blocks):
            a = jnp.exp(z - sp + (loc + carry))
            if m is not None:
                a = jnp.where(m, a, 0.0)
            carry = carry + (loc[:, 0:1] + ln[:, 0:1])
            a_hi, a_lo = _split(a, 2)
            v_hi, v_lo = _split(v, 2)
            d = av(jnp.concatenate([a_hi, a_lo], axis=0), v_hi)
            acc = acc + (d[:R] + d[R:] + av(a_hi, v_lo))
        return carry, acc

    @pl.when(g == 0)
    def _():
        pad = jnp.zeros((PAGE_SIZE - T, SB_W), F32)
        k_new = jnp.concatenate([kn_ref[0], pad], axis=0)
        v_new = jnp.concatenate([vn_ref[0], pad], axis=0)
        tok = _imod(lax.broadcasted_iota(jnp.int32, (R, PAGE_SIZE), 0), T)
        key = lax.broadcasted_iota(jnp.int32, (R, PAGE_SIZE), 1)
        carry, acc = run([(k_new, v_new, key < tok)], jnp.zeros((R, 1), F32), jnp.zeros((R, SB_W), F32), False)
        carry_scr[...] = carry
        acc_scr[...] = acc

    blocks = [(k_pages[j][0, 0], v_pages[j][0, 0], None) for j in reversed(range(P))]
    carry, acc = run(blocks, carry_scr[...], acc_scr[...], True)
    carry_scr[...] = carry
    acc_scr[...] = acc

    @pl.when(g == pl.num_programs(1) - 1)
    def _():
        mine = jnp.where(own, acc_scr[...], 0.0)
        out = mine[0:T]
        for h in range(1, SB_HEADS):
            out = out + mine[h * T:(h + 1) * T]
        o_ref[0] = out


def _sb_sample(q, k_new, v_new, cache_k, cache_v, page_table_flat, bias_rows, layer, n_pages):
    DB, T, _ = q.shape
    P = SB_PAGES_PER_STEP
    n_steps = n_pages // P
    tok = pl.BlockSpec((1, T, SB_W), lambda b, g, pt: (b, 0, 0))

    def page_spec(j):
        return pl.BlockSpec((1, 1, SB_W, PAGE_SIZE),
                            lambda b, g, pt, j=j: (layer, pt[b * n_pages + (n_steps - 1 - g) * P + j], 0, 0))

    grid_spec = pltpu.PrefetchScalarGridSpec(
        num_scalar_prefetch=1,
        grid=(DB, n_steps),
        in_specs=[tok, tok, tok, pl.BlockSpec((SB_HEADS * T, PAGE_SIZE), lambda b, g, pt: (0, 0))]
        + [page_spec(j) for j in range(P)] * 2,
        out_specs=tok,
        scratch_shapes=[pltpu.VMEM((SB_HEADS * T, 1), F32), pltpu.VMEM((SB_HEADS * T, SB_W), F32)],
    )
    return pl.pallas_call(
        functools.partial(_sb_sample_kernel, pages_per_step=P),
        grid_spec=grid_spec,
        out_shape=jax.ShapeDtypeStruct(q.shape, F32),
        compiler_params=_params(("arbitrary", "arbitrary")),
        name="sb_sample",
    )(page_table_flat, q, k_new, v_new, bias_rows, *([cache_k] * P), *([cache_v] * P))


def _route(logits):
    lane_i = lax.broadcasted_iota(jnp.int32, logits.shape, 1)
    lane = lane_i.astype(F32)
    neg = -jnp.inf
    none = float(ROUTER_LANES)
    is_g = lane_i < N_GROUPS
    lg = jnp.where(is_g, logits, neg)
    mg = jnp.max(lg, axis=-1, keepdims=True)
    pg_top = 1.0 / jnp.sum(jnp.where(is_g, jnp.exp(lg - mg), 0.0), axis=-1, keepdims=True)
    g_idx = jnp.min(jnp.where(lg == mg, lane, none), axis=-1, keepdims=True)
    e_rel = lane_i - EXPERT_LANE0
    lane_group = _idiv(e_rel, EXPERTS_PER_GROUP).astype(F32)
    in_group = jnp.logical_and(jnp.logical_and(e_rel >= 0, e_rel < N_EXPERTS), lane_group == g_idx)
    el = jnp.where(in_group, logits, neg)
    m1 = jnp.max(el, axis=-1, keepdims=True)
    i1 = jnp.min(jnp.where(el == m1, lane, none), axis=-1, keepdims=True)
    el2 = jnp.where(lane == i1, neg, el)
    m2 = jnp.max(el2, axis=-1, keepdims=True)
    i2 = jnp.min(jnp.where(el2 == m2, lane, none), axis=-1, keepdims=True)
    t = jnp.exp(m2 - m1)
    w1 = 1.0 / (1.0 + t)
    w2 = t / (1.0 + t)
    return jnp.where(lane == i1, pg_top * w1, 0.0) + jnp.where(lane == i2, pg_top * w2, 0.0)


def _outmoe_kernel(x_ref, yr_ref, yg_ref, ys_ref, g1_ref, sc2_ref, sh2_ref, g2_ref, n2g_ref, fg_ref, wout_ref,
                   wr_ref, br_ref, wg_ref, wu_ref, wd_ref, o_ref, x1_scr, h2_scr, dw_scr, acc_scr, *, final, hp):
    nb, tt, d = x_ref.shape
    rows = nb * tt
    e = pl.program_id(2)

    @pl.when(e == 0)
    def _():
        flat = lambda ref: ref[...].reshape(rows, ref.shape[-1])
        mixed = (_dot(flat(yr_ref), wout_ref[0:RWKV_W], hp)
                 + _dot(flat(yg_ref), wout_ref[RWKV_W:RWKV_W + GMLP_W], hp)
                 + _dot(flat(ys_ref), wout_ref[RWKV_W + GMLP_W:], hp))
        x1 = x_ref[...] + g1_ref[...] * mixed.reshape(nb, tt, d)
        h2 = _rms_mod(x1, n2g_ref[...], sc2_ref[...], sh2_ref[...]).reshape(rows, d)
        x1_scr[...] = x1
        h2_scr[...] = h2.astype(BF16)
        logits = jnp.dot(h2, wr_ref[...], preferred_element_type=F32, precision=lax.Precision.HIGHEST) + br_ref[...]
        dw_scr[...] = _route(logits)
        acc_scr[...] = jnp.zeros_like(acc_scr)

    hb = h2_scr[...]
    gate = jnp.dot(hb, wg_ref[0, 0], preferred_element_type=F32)
    up = jnp.dot(hb, wu_ref[0, 0], preferred_element_type=F32)
    lane = lax.broadcasted_iota(jnp.int32, (rows, ROUTER_LANES), 1)
    w_e = jnp.sum(jnp.where(lane == e + EXPERT_LANE0, dw_scr[...], 0.0), axis=-1, keepdims=True)
    hid = gate * _sigmoid(gate) * up * w_e
    acc_scr[...] += jnp.dot(hid.astype(BF16), wd_ref[0, 0], preferred_element_type=F32)

    @pl.when(e == pl.num_programs(2) - 1)
    def _():
        x2 = x1_scr[...] + g2_ref[...] * acc_scr[...].reshape(nb, tt, d)
        if final:
            x2 = x2 * lax.rsqrt(jnp.mean(x2 * x2, axis=-1, keepdims=True) + RMS_EPS) * fg_ref[...]
        o_ref[...] = x2


def _out_moe(x, y_rwkv, y_gmlp, y_sb, gate1, scale2, shift2, gate2, lw, layer, final_g, final, nb, tt, hp):
    B, T, _ = x.shape
    rows = nb * tt
    tok = lambda w: pl.BlockSpec((nb, tt, w), lambda i, j, e: (i, j, 0))
    per_seq = pl.BlockSpec((nb, 1, D_MODEL), lambda i, j, e: (i, 0, 0))
    vec = pl.BlockSpec((1, 1, D_MODEL), lambda i, j, e: (0, 0, 0))
    return pl.pallas_call(
        functools.partial(_outmoe_kernel, final=final, hp=hp),
        grid=(B // nb, T // tt, N_EXPERTS),
        in_specs=[tok(D_MODEL), tok(RWKV_W), tok(GMLP_W), tok(SB_W), per_seq, per_seq, per_seq, per_seq, vec, vec,
                  pl.BlockSpec((D_MODEL, D_MODEL), lambda i, j, e: (0, 0)),
                  pl.BlockSpec((D_MODEL, ROUTER_LANES), lambda i, j, e: (0, 0)),
                  pl.BlockSpec((1, ROUTER_LANES), lambda i, j, e: (0, 0)),
                  pl.BlockSpec((1, 1, D_MODEL, D_EXPERT), lambda i, j, e: (layer, e, 0, 0)),
                  pl.BlockSpec((1, 1, D_MODEL, D_EXPERT), lambda i, j, e: (layer, e, 0, 0)),
                  pl.BlockSpec((1, 1, D_EXPERT, D_MODEL), lambda i, j, e: (layer, e, 0, 0))],
        out_specs=tok(D_MODEL),
        out_shape=jax.ShapeDtypeStruct(x.shape, F32),
        scratch_shapes=[pltpu.VMEM((nb, tt, D_MODEL), F32), pltpu.VMEM((rows, D_MODEL), BF16),
                        pltpu.VMEM((rows, ROUTER_LANES), F32), pltpu.VMEM((rows, D_MODEL), F32)],
        compiler_params=_params(("arbitrary", "arbitrary", "arbitrary")),
        name="out_moe",
    )(x, y_rwkv, y_gmlp, y_sb, gate1, scale2, shift2, gate2, lw['norm2_g'], final_g,
      lw['w_out_f32'] if hp else lw['w_out'], lw['w_router'], lw['b_router'], lw['exp_wg'], lw['exp_wu'], lw['exp_wd'])


def _trunk_layer(x, mod, lw, layer, s0_full, prev0, gm_ws, gm_bias, sb_fn, final_g, final, nb_tok_in, tt_in,
                 nb_tok_out, tt_out, rwkv_ns, rwkv_ts, hp):
    B, T, _ = x.shape
    shift1, scale1, gate1, shift2, scale2, gate2 = mod
    p_rwkv, g_u, g_v, q, k, v = _in_projection(x, scale1, shift1, lw['norm1_g'],
                                               lw['w_in_f32'] if hp else lw['w_in'], nb_tok_in, tt_in, hp)
    y_rwkv, s_new = _rwkv_mix(p_rwkv, prev0, s0_full, lw['rwkv'], rwkv_ns, rwkv_ts, hp)
    n_chunks = B * T // CHUNK
    y_gmlp, v_rows = _gmlp_mix(g_u.reshape(n_chunks, CHUNK, GMLP_W), g_v.reshape(n_chunks, CHUNK, GMLP_W),
                               lw['gmlp_ln_g'], lw['gmlp_ln_b'], gm_ws, gm_bias, hp)
    y_gmlp = y_gmlp.reshape(B, T, GMLP_W)
    v_rows = v_rows.reshape(B, T, GMLP_W)
    y_sb = sb_fn(q, k, v)
    x_out = _out_moe(x, y_rwkv, y_gmlp, y_sb, gate1, scale2, shift2, gate2, lw, layer, final_g, final,
                     nb_tok_out, tt_out, hp)
    return x_out, s_new, p_rwkv, k, v, v_rows


def _pad_rows(w, row0):
    return jnp.zeros((LORA_W, RWKV_W), F32).at[row0:row0 + w.shape[0]].set(w)


def kernel(x_prompt, x_sample, cache_k, cache_v, page_table, state_rwkv, state_rwkv_shift, c_prompt, c_sample, norm1_g, norm2_g, w_mod, b_mod, w_in, rwkv_mu, rwkv_w0, rwkv_w_up, rwkv_a0, rwkv_a_up, rwkv_g_up, rwkv_k_k, rwkv_k_a, rwkv_r_k, rwkv_ln_g, rwkv_ln_b, gmlp_ln_g, gmlp_ln_b, gmlp_ws, gmlp_bs, sb_bias, w_out, router_g_w, router_g_b, router_e_w, router_e_b, exp_wg, exp_wu, exp_wd, final_g):
    B, T, _ = x_prompt.shape
    DB, TS, _ = x_sample.shape
    depth = w_in.shape[0]
    n_pages = page_table.shape[1]
    n_pool = cache_k.shape[1]

    mod_all = _modulation(jnp.concatenate([c_prompt, c_sample], axis=0), w_mod, b_mod)
    cache_k4 = jnp.transpose(cache_k, (0, 1, 3, 4, 2)).reshape(depth, n_pool, SB_W, PAGE_SIZE)
    cache_v4 = jnp.transpose(cache_v, (0, 1, 3, 4, 2)).reshape(depth, n_pool, SB_W, PAGE_SIZE)
    pt_flat = page_table.reshape(-1)
    exp_wg_b = exp_wg.astype(BF16)
    exp_wu_b = exp_wu.astype(BF16)
    exp_wd_b = exp_wd.astype(BF16)
    fg = final_g.reshape(1, 1, D_MODEL)
    seqs_per_chunk = CHUNK // TS

    xp, xs = x_prompt, x_sample
    outs = {n: [] for n in ('kp', 'vp', 'ks', 'vs', 'Sp', 'shp', 'Ss', 'shs', 'gv')}
    for l in range(depth):
        w_router = jnp.zeros((D_MODEL, ROUTER_LANES), F32)
        w_router = w_router.at[:, :N_GROUPS].set(router_g_w[l]).at[:, EXPERT_LANE0:EXPERT_LANE0 + N_EXPERTS].set(router_e_w[l])
        b_router = jnp.zeros((1, ROUTER_LANES), F32)
        b_router = b_router.at[0, :N_GROUPS].set(router_g_b[l]).at[0, EXPERT_LANE0:EXPERT_LANE0 + N_EXPERTS].set(router_e_b[l])
        row = lambda a: a.reshape(1, -1)
        lw = dict(
            norm1_g=norm1_g[l], norm2_g=norm2_g[l].reshape(1, 1, D_MODEL), w_in=w_in[l].astype(BF16),
            w_in_f32=w_in[l], w_out=w_out[l].astype(BF16), w_out_f32=w_out[l],
            w_router=w_router, b_router=b_router,
            exp_wg=exp_wg_b, exp_wu=exp_wu_b, exp_wd=exp_wd_b,
            gmlp_ln_g=row(gmlp_ln_g[l]), gmlp_ln_b=row(gmlp_ln_b[l]),
            rwkv=dict(mu=row(rwkv_mu[l]), w0=row(rwkv_w0[l]), w_up=_pad_rows(rwkv_w_up[l], 0), a0=row(rwkv_a0[l]),
                      a_up=_pad_rows(rwkv_a_up[l], W_LORA), g_up=_pad_rows(rwkv_g_up[l], W_LORA + A_LORA),
                      k_k=row(rwkv_k_k[l]), k_a=row(rwkv_k_a[l]), r_k=row(rwkv_r_k[l]), ln_g=row(rwkv_ln_g[l]),
                      ln_b=row(rwkv_ln_b[l])),
        )
        final = l == depth - 1
        mods_p = [mod_all[l, i, :B].reshape(B, 1, D_MODEL) for i in range(N_MOD)]
        mods_s = [mod_all[l, i, B:].reshape(DB, 1, D_MODEL) for i in range(N_MOD)]

        bias_p = jnp.repeat(jnp.swapaxes(gmlp_bs[l], 0, 1), HEAD_DIM, axis=1)

        def sb_prompt(q, k, v, l=l):
            hm = lambda a: jnp.transpose(a.reshape(B, T, SB_HEADS, HEAD_DIM), (0, 2, 1, 3))
            o = _sb_prompt(hm(q), hm(k), hm(v), sb_bias[l])
            return jnp.transpose(o, (0, 2, 1, 3)).reshape(B, T, SB_W)

        xp, Sp, pp, kp, vp, _ = _trunk_layer(
            xp, mods_p, lw, l, jnp.zeros((B, RWKV_W, RWKV_W), F32), jnp.zeros((B, 1, RWKV_PROJ), F32),
            gmlp_ws[l], bias_p, sb_prompt, fg, final, 1, TOK_TILE_IN, 1, TOK_TILE_OUT, RWKV_SEQS_PER_STEP,
            RWKV_CHUNK, False)

        ws_s = jnp.einsum('ab,gts->gatbs', jnp.eye(seqs_per_chunk, dtype=F32), gmlp_ws[l][:, :TS, :TS])
        ws_s = ws_s.reshape(GMLP_GROUPS, CHUNK, CHUNK)
        bias_s = jnp.tile(bias_p[:TS], (seqs_per_chunk, 1))
        bias_rows = jnp.broadcast_to(jnp.repeat(sb_bias[l], TS)[:, None], (SB_HEADS * TS, PAGE_SIZE))

        def sb_sample(q, k, v, l=l, bias_rows=bias_rows):
            return _sb_sample(q, k, v, cache_k4, cache_v4, pt_flat, bias_rows, l, n_pages)

        xs, Ss, ps_, ksm, vsm, gv = _trunk_layer(
            xs, mods_s, lw, l, _state_to_full(state_rwkv[l]), state_rwkv_shift[l].reshape(DB, 1, RWKV_PROJ),
            ws_s, bias_s, sb_sample, fg, final, TOK_TILE_SAMPLE // TS, TS, TOK_TILE_OUT // TS, TS,
            RWKV_CHUNK // TS, TS, True)

        outs['kp'].append(kp.reshape(B, T, SB_HEADS, HEAD_DIM))
        outs['vp'].append(vp.reshape(B, T, SB_HEADS, HEAD_DIM))
        outs['ks'].append(ksm.reshape(DB, TS, SB_HEADS, HEAD_DIM))
        outs['vs'].append(vsm.reshape(DB, TS, SB_HEADS, HEAD_DIM))
        outs['Sp'].append(_state_from_full(Sp))
        outs['shp'].append(pp[:, -1])
        outs['Ss'].append(_state_from_full(Ss))
        outs['shs'].append(ps_[:, -1])
        outs['gv'].append(gv)

    st = lambda n: jnp.stack(outs[n])
    return (xp, xs, st('kp'), st('vp'), st('ks'), st('vs'), st('Sp'), st('shp'), st('Ss'), st('shs'), st('gv'))
```

```python
import functools

import jax
import jax.numpy as jnp
from jax import lax
from jax.experimental import pallas as pl
from jax.experimental.pallas import tpu as pltpu

F32 = jnp.float32
BF16 = jnp.bfloat16

D_MODEL = 1024
HEAD_DIM = 64
RWKV_HEADS = 4
RWKV_W = RWKV_HEADS * HEAD_DIM
LORA_W = 128
RWKV_PROJ = 3 * RWKV_W + LORA_W
W_LORA, A_LORA, G_LORA = 32, 32, 64
GN_EPS = 64e-5
GMLP_GROUPS = 4
GMLP_W = GMLP_GROUPS * HEAD_DIM
CHUNK = 128
LN_EPS = 1e-5
SB_HEADS = 8
SB_W = SB_HEADS * HEAD_DIM
PAGE_SIZE = 128
N_GROUPS = 4
EXPERTS_PER_GROUP = 4
N_EXPERTS = N_GROUPS * EXPERTS_PER_GROUP
D_EXPERT = 256
N_MOD = 6
RMS_EPS = 1e-6
SEG_WIDTHS = (RWKV_PROJ, GMLP_W, GMLP_W, SB_W, SB_W, SB_W)

LANES = 128
ROUTER_LANES = LANES
EXPERT_LANE0 = N_GROUPS
VMEM_LIMIT = 56 * 1024 * 1024

RWKV_CHUNK = 64
SB_BLOCK = 256
SB_STRIP = 64
RWKV_SEQS_PER_STEP = 4
EXPERTS_PER_STEP = 4
SB_PAGES_PER_STEP = 16
TOK_TILE_IN = 512
TOK_TILE_SAMPLE = 256
TOK_TILE_OUT = 512
GMLP_CHUNKS_PER_STEP = 4


def _params(sem):
    return pltpu.CompilerParams(dimension_semantics=sem, vmem_limit_bytes=VMEM_LIMIT)


def _mm(a, b, dims, hp):
    dg = lambda x, y: lax.dot_general(x, y, (dims, ((), ())), preferred_element_type=F32)
    if not hp:
        return dg(a.astype(BF16), b.astype(BF16))
    a_hi, a_lo = _split(a.astype(F32), 2)
    b_hi, b_lo = _split(b.astype(F32), 2)
    return dg(a_hi, b_hi) + dg(a_hi, b_lo) + dg(a_lo, b_hi)


def _dot(a, b, hp=False):
    return _mm(a, b, ((1,), (0,)), hp)


def _dot_nt(a, b, hp=False):
    return _mm(a, b, ((1,), (1,)), hp)


def _dot_tn(a, b, hp=False):
    return _mm(a, b, ((0,), (0,)), hp)


def _split(x, n):
    terms = []
    rem = x
    for _ in range(n):
        t = rem.astype(BF16)
        terms.append(t)
        rem = rem - t.astype(F32)
    return terms


def _dot_exact_rhs(x, m, n=2):
    out = None
    for t in _split(x, n):
        d = jnp.dot(t, m, preferred_element_type=F32)
        out = d if out is None else out + d
    return out


def _dot_exact_lhs(m, x, n=2):
    out = None
    for t in _split(x, n):
        d = jnp.dot(m, t, preferred_element_type=F32)
        out = d if out is None else out + d
    return out


def _softplus(z):
    return jnp.maximum(z, 0.0) + jnp.log1p(jnp.exp(-jnp.abs(z)))


def _sigmoid(z):
    return 1.0 / (1.0 + jnp.exp(-z))


def _gelu_tanh(x):
    return 0.5 * x * (1.0 + jnp.tanh(0.7978845608028654 * (x + 0.044715 * (x * x * x))))


def _idiv(x, n):
    assert n & (n - 1) == 0
    return jnp.right_shift(x, n.bit_length() - 1)


def _imod(x, n):
    assert n & (n - 1) == 0
    return jnp.bitwise_and(x, n - 1)


def _head_block_mask(n, dtype=BF16):
    r = _idiv(lax.broadcasted_iota(jnp.int32, (n, n), 0), HEAD_DIM)
    c = _idiv(lax.broadcasted_iota(jnp.int32, (n, n), 1), HEAD_DIM)
    return jnp.where(r == c, 1.0, 0.0).astype(dtype)


def _mod_kernel(c_ref, w_ref, b_ref, o_ref):
    c = c_ref[...]
    o_ref[0, 0] = _dot(c * _sigmoid(c), w_ref[0], hp=True) + b_ref[0, 0]


def _modulation(c_all, w_mod, b_mod):
    depth = w_mod.shape[0]
    rows = c_all.shape[0]
    return pl.pallas_call(
        _mod_kernel,
        grid=(depth, N_MOD),
        in_specs=[
            pl.BlockSpec((rows, D_MODEL), lambda l, j: (0, 0)),
            pl.BlockSpec((1, D_MODEL, D_MODEL), lambda l, j: (l, 0, j)),
            pl.BlockSpec((1, 1, 1, D_MODEL), lambda l, j: (l, j, 0, 0)),
        ],
        out_specs=pl.BlockSpec((1, 1, rows, D_MODEL), lambda l, j: (l, j, 0, 0)),
        out_shape=jax.ShapeDtypeStruct((depth, N_MOD, rows, D_MODEL), F32),
        compiler_params=_params(("arbitrary", "arbitrary")),
        name="modulation",
    )(c_all, w_mod, b_mod.reshape(depth, N_MOD, 1, D_MODEL))


def _rms_mod(x, g, scale, shift):
    y = x * lax.rsqrt(jnp.mean(x * x, axis=-1, keepdims=True) + RMS_EPS) * g
    return y * (1.0 + scale) + shift


def _inproj_kernel(x_ref, sc_ref, sh_ref, g_ref, w_ref, *o_refs, hp):
    nb, tt, d = x_ref.shape
    h = _rms_mod(x_ref[...], g_ref[...], sc_ref[...], sh_ref[...]).reshape(nb * tt, d)
    if not hp:
        h = h.astype(BF16)
    off = 0
    for o_ref in o_refs:
        w = o_ref.shape[-1]
        o_ref[...] = _dot(h, w_ref[:, off:off + w], hp).reshape(nb, tt, w)
        off += w


def _in_projection(x, scale, shift, g, w_in, nb, tt, hp):
    B, T, _ = x.shape
    p_total = w_in.shape[1]
    tok = lambda w: pl.BlockSpec((nb, tt, w), lambda i, j: (i, j, 0))
    per_seq = pl.BlockSpec((nb, 1, D_MODEL), lambda i, j: (i, 0, 0))
    return pl.pallas_call(
        functools.partial(_inproj_kernel, hp=hp),
        grid=(B // nb, T // tt),
        in_specs=[tok(D_MODEL), per_seq, per_seq,
                  pl.BlockSpec((1, 1, D_MODEL), lambda i, j: (0, 0, 0)),
                  pl.BlockSpec((D_MODEL, p_total), lambda i, j: (0, 0))],
        out_specs=[tok(w) for w in SEG_WIDTHS],
        out_shape=[jax.ShapeDtypeStruct((B, T, w), F32) for w in SEG_WIDTHS],
        compiler_params=_params(("arbitrary", "arbitrary")),
        name="in_projection",
    )(x, scale, shift, g.reshape(1, 1, D_MODEL), w_in)


def _rwkv_kernel(p_ref, prev0_ref, s0_ref, mu_ref, w0_ref, wup_ref, a0_ref, aup_ref, gup_ref, kk_ref, ka_ref,
                 rk_ref, lng_ref, lnb_ref, y_ref, s_out_ref, s_scr, carry_scr, *, ns, ts, hp):
    C = ns * ts
    W = RWKV_W
    chunk = pl.program_id(1)
    dot = functools.partial(_dot, hp=hp)
    dot_nt = functools.partial(_dot_nt, hp=hp)
    dot_tn = functools.partial(_dot_tn, hp=hp)

    @pl.when(chunk == 0)
    def _():
        s_scr[...] = s0_ref[...]
        carry_scr[...] = prev0_ref[...]

    p = p_ref[...].reshape(C, RWKV_PROJ)
    row1 = lax.broadcasted_iota(jnp.int32, (C, 1), 0)
    prev_first = jnp.broadcast_to(carry_scr[...], (ns, ts, RWKV_PROJ)).reshape(C, RWKV_PROJ)
    prev = jnp.where(_imod(row1, ts) == 0, prev_first, pltpu.roll(p, 1, 0))
    carry_scr[...] = p_ref[:, ts - 1:ts, :]
    ps = p + mu_ref[...] * (prev - p)

    r = ps[:, 0:W]
    k = ps[:, W:2 * W]
    v = ps[:, 2 * W:3 * W]
    lora = ps[:, 3 * W:]
    w = -_softplus(-(w0_ref[...] + dot(jnp.tanh(lora), wup_ref[...]))) - 0.5
    lw = -jnp.exp(w)
    a = _sigmoid(a0_ref[...] + dot(lora, aup_ref[...]))
    g = dot(_sigmoid(lora), gup_ref[...])

    head_bd = _head_block_mask(W)
    kkr = k * kk_ref[...]
    kk = kkr * lax.rsqrt(jnp.maximum(_dot_exact_rhs(kkr * kkr, head_bd), 1e-24))
    kh = k * (1.0 + (a - 1.0) * ka_ref[...])
    a_vec = -kk
    b_vec = kk * a

    rowi = lax.broadcasted_iota(jnp.int32, (C, C), 0)
    coli = lax.broadcasted_iota(jnp.int32, (C, C), 1)
    same = _idiv(rowi, ts) == _idiv(coli, ts)
    strict = jnp.logical_and(same, coli < rowi)
    incl = jnp.logical_and(same, coli <= rowi)
    L = _dot_exact_lhs(jnp.where(incl, 1.0, 0.0).astype(BF16), lw, 3)
    L_end = _dot_exact_lhs(jnp.where(same, 1.0, 0.0).astype(BF16), lw, 3)
    e_neg = jnp.exp(-L)
    e_end = jnp.exp(L_end - L)
    At = a_vec * jnp.exp(L - lw)
    Rt = r * jnp.exp(L)
    Bt = b_vec * e_neg
    Kt = kh * e_neg
    Bh = b_vec * e_end
    Kh = kh * e_end
    P_end = jnp.exp(L_end)

    lane_head = _idiv(lax.broadcasted_iota(jnp.int32, (1, W), 1), HEAD_DIM)
    eye =jnp.where(rowi == coli, 1.0, 0.0)
    n_factors = max(1, (ts - 1).bit_length())
    T_heads, rb_heads = [], []
    akv = jnp.zeros((C, W), F32)
    rkv = jnp.zeros((C, W), F32)
    for h in range(RWKV_HEADS):
        mh = lane_head == h
        At_h = jnp.where(mh, At, 0.0)
        Rt_h = jnp.where(mh, Rt, 0.0)
        ab = jnp.where(strict, dot_nt(At_h, Bt), 0.0)
        ak = jnp.where(strict, dot_nt(At_h, Kt), 0.0)
        rb = jnp.where(incl, dot_nt(Rt_h, Bt), 0.0)
        rk = jnp.where(incl, dot_nt(Rt_h, Kt), 0.0)
        T = eye + ab
        apow = ab
        for _ in range(n_factors - 1):
            apow = dot(apow, apow)
            T = T + dot(T, apow)
        akv = akv + jnp.where(mh, dot(ak, v), 0.0)
        rkv = rkv + jnp.where(mh, dot(rk, v), 0.0)
        T_heads.append(T)
        rb_heads.append(rb)

    xs, ys = [], []
    for q in range(ns):
        rows = slice(q * ts, (q + 1) * ts)
        Sq = s_scr[q]
        xs.append(dot_nt(At[rows], Sq))
        ys.append(dot_nt(Rt[rows], Sq))
    X = xs[0] if ns == 1 else jnp.concatenate(xs, axis=0)
    YS = ys[0] if ns == 1 else jnp.concatenate(ys, axis=0)
    rhs = X + akv
    U = jnp.zeros((C, W), F32)
    for h in range(RWKV_HEADS):
        U = U + jnp.where(lane_head == h, dot(T_heads[h], rhs), 0.0)
    Y = YS + rkv
    for h in range(RWKV_HEADS):
        Y = Y + jnp.where(lane_head == h, dot(rb_heads[h], U), 0.0)

    bd_f32 = _head_block_mask(W, F32)
    for q in range(ns):
        rows = slice(q * ts, (q + 1) * ts)
        upd = dot_tn(U[rows], Bh[rows]) + dot_tn(v[rows], Kh[rows])
        s_scr[q] = (s_scr[q] * P_end[q * ts:q * ts + 1, :] + upd) * bd_f32

    inv_n = 1.0 / HEAD_DIM
    mean = _dot_exact_rhs(Y, head_bd) * inv_n
    yc = Y - mean
    var = _dot_exact_rhs(yc * yc, head_bd) * inv_n
    yn = yc * lax.rsqrt(var + GN_EPS) * lng_ref[...] + lnb_ref[...]
    bonus = _dot_exact_rhs(r * kh * rk_ref[...], head_bd) * v
    y_ref[...] = ((yn + bonus) * g).reshape(ns, ts, W)

    @pl.when(chunk == pl.num_programs(1) - 1)
    def _():
        s_out_ref[...] = s_scr[...]


def _rwkv_mix(p_rwkv, prev0, s0_full, lw, ns, ts, hp):
    B, T, _ = p_rwkv.shape
    W = RWKV_W
    vec = lambda n: pl.BlockSpec((1, n), lambda i, j: (0, 0))
    mat = lambda: pl.BlockSpec((LORA_W, W), lambda i, j: (0, 0))
    kern = functools.partial(_rwkv_kernel, ns=ns, ts=ts, hp=hp)
    return pl.pallas_call(
        kern,
        grid=(B // ns, T // ts),
        in_specs=[
            pl.BlockSpec((ns, ts, RWKV_PROJ), lambda i, j: (i, j, 0)),
            pl.BlockSpec((ns, 1, RWKV_PROJ), lambda i, j: (i, 0, 0)),
            pl.BlockSpec((ns, W, W), lambda i, j: (i, 0, 0)),
            vec(RWKV_PROJ), vec(W), mat(), vec(W), mat(), mat(), vec(W), vec(W), vec(W), vec(W), vec(W),
        ],
        out_specs=[
            pl.BlockSpec((ns, ts, W), lambda i, j: (i, j, 0)),
            pl.BlockSpec((ns, W, W), lambda i, j: (i, 0, 0)),
        ],
        out_shape=[jax.ShapeDtypeStruct((B, T, W), F32), jax.ShapeDtypeStruct((B, W, W), F32)],
        scratch_shapes=[pltpu.VMEM((ns, W, W), F32), pltpu.VMEM((ns, 1, RWKV_PROJ), F32)],
        compiler_params=_params(("arbitrary", "arbitrary")),
        name="rwkv_mix",
    )(p_rwkv, prev0, s0_full, lw['mu'], lw['w0'], lw['w_up'], lw['a0'], lw['a_up'], lw['g_up'],
      lw['k_k'], lw['k_a'], lw['r_k'], lw['ln_g'], lw['ln_b'])


def _state_to_full(s):
    B = s.shape[0]
    eye = jnp.eye(RWKV_HEADS, dtype=s.dtype)
    return jnp.einsum('bhij,hg->bhigj', s, eye).reshape(B, RWKV_W, RWKV_W)


def _state_from_full(sf):
    B = sf.shape[0]
    s5 = sf.reshape(B, RWKV_HEADS, HEAD_DIM, RWKV_HEADS, HEAD_DIM)
    return jnp.stack([s5[:, h, :, h, :] for h in range(RWKV_HEADS)], axis=1)


def _gmlp_kernel(u_ref, v_ref, lng_ref, lnb_ref, ws_ref, bias_ref, y_ref, vrow_ref, *, hp):
    W = GMLP_W
    head_bd = _head_block_mask(W)
    inv_n = 1.0 / HEAD_DIM
    row = lax.broadcasted_iota(jnp.int32, (CHUNK, CHUNK), 0)
    col = lax.broadcasted_iota(jnp.int32, (CHUNK, CHUNK), 1)
    tril = col <= row
    lane_group = _idiv(lax.broadcasted_iota(jnp.int32, (1, W), 1), HEAD_DIM)
    wms = [jnp.where(tril, ws_ref[gi], 0.0) for gi in range(GMLP_GROUPS)]
    if not hp:
        wms = [wm.astype(BF16) for wm in wms]
    for ci in range(u_ref.shape[0]):
        u = _gelu_tanh(u_ref[ci])
        v = _gelu_tanh(v_ref[ci])
        mean = _dot_exact_rhs(v, head_bd) * inv_n
        vc = v - mean
        var = _dot_exact_rhs(vc * vc, head_bd) * inv_n
        vn = vc * lax.rsqrt(var + LN_EPS) * lng_ref[...] + lnb_ref[...]
        vrow_ref[ci] = vn
        mixed = bias_ref[...]
        for gi in range(GMLP_GROUPS):
            mixed = mixed + _dot(wms[gi], jnp.where(lane_group == gi, vn, 0.0), hp)
        y_ref[ci] = u * mixed


def _gmlp_mix(u, v, ln_g, ln_b, ws_chunk, bias_chunk, hp):
    n = u.shape[0]
    nc = GMLP_CHUNKS_PER_STEP
    blk = pl.BlockSpec((nc, CHUNK, GMLP_W), lambda i: (i, 0, 0))
    vec = pl.BlockSpec((1, GMLP_W), lambda i: (0, 0))
    return pl.pallas_call(
        functools.partial(_gmlp_kernel, hp=hp),
        grid=(n // nc,),
        in_specs=[blk, blk, vec, vec,
                  pl.BlockSpec((GMLP_GROUPS, CHUNK, CHUNK), lambda i: (0, 0, 0)),
                  pl.BlockSpec((CHUNK, GMLP_W), lambda i: (0, 0))],
        out_specs=[blk, blk],
        out_shape=[jax.ShapeDtypeStruct(u.shape, F32), jax.ShapeDtypeStruct(u.shape, F32)],
        compiler_params=_params(("arbitrary",)),
        name="gmlp_mix",
    )(u, v, ln_g, ln_b, ws_chunk, bias_chunk)


def _sb_block(qb, kb, vb, bias, U, carry, acc, mask, hp=False):
    m = qb.shape[0]
    z = _dot_nt(qb, kb, hp) + bias
    sp = jnp.maximum(z, 0.0) + jnp.log(1.0 + jnp.exp(-jnp.abs(z)))
    log_not = -sp
    if mask is not None:
        log_not = jnp.where(mask, log_not, 0.0)
    hi, lo = _split(log_not, 2)
    both = jnp.dot(jnp.concatenate([hi, lo], axis=0), U, preferred_element_type=F32)
    local = both[:m] + both[m:]
    a = jnp.exp(z - sp + (local + carry))
    if mask is not None:
        a = jnp.where(mask, a, 0.0)
    acc = acc + _dot(a, vb, hp)
    carry = carry + (local[:, 0:1] + log_not[:, 0:1])
    return carry, acc


def _suffix_matrix(n):
    r = lax.broadcasted_iota(jnp.int32, (n, n), 0)
    c = lax.broadcasted_iota(jnp.int32, (n, n), 1)
    return jnp.where(r > c, 1.0, 0.0).astype(BF16)


def _sb_prompt_kernel(bias_ref, q_ref, k_ref, v_ref, o_ref, *, blk, strip):
    h = pl.program_id(1)
    bias = bias_ref[h]
    n_strips = blk // strip
    U = _suffix_matrix(blk)
    row = lax.broadcasted_iota(jnp.int32, (strip, blk), 0)
    col = lax.broadcasted_iota(jnp.int32, (strip, blk), 1)

    def kv(j):
        start = pl.multiple_of(j * blk, blk)
        return k_ref[0, 0, pl.ds(start, blk), :].astype(BF16), v_ref[0, 0, pl.ds(start, blk), :].astype(BF16)

    def rows(qi, s):
        return pl.ds(pl.multiple_of(qi * blk + s * strip, strip), strip)

    def query_block(qi, unused):
        qs = [(q_ref[0, 0, rows(qi, s), :] * (HEAD_DIM ** -0.5)).astype(BF16) for s in range(n_strips)]

        def sweep(js, state, diag_first):
            units = [(t, s) for t in range(len(js)) for s in range(n_strips)]
            kvs = [kv(j) for j in js]
            masks = [(col < row + s * strip) if (diag_first and t == 0) else None for t, s in units]
            zs = [_dot_nt(qs[s], kvs[t][0]) + bias for t, s in units]
            sps = [jnp.maximum(z, 0.0) + jnp.log(1.0 + jnp.exp(-jnp.abs(z))) for z in zs]
            lns = [-sp if m is None else jnp.where(m, -sp, 0.0) for sp, m in zip(sps, masks)]
            boths = [jnp.dot(jnp.concatenate(_split(ln, 2), axis=0), U, preferred_element_type=F32) for ln in lns]
            locs = [b[:strip] + b[strip:] for b in boths]
            state = list(state)
            for u, (t, s) in enumerate(units):
                a = jnp.exp(zs[u] - sps[u] + (locs[u] + state[2 * s]))
                if masks[u] is not None:
                    a = jnp.where(masks[u], a, 0.0)
                state[2 * s] = state[2 * s] + (locs[u][:, 0:1] + lns[u][:, 0:1])
                state[2 * s + 1] = state[2 * s + 1] + _dot(a, kvs[t][1])
            return tuple(state)

        state = tuple(jnp.zeros((strip, w), F32) for _ in range(n_strips) for w in (1, HEAD_DIM))
        odd = qi % 2
        state = lax.cond(odd == 1, lambda st: sweep([qi, qi - 1], st, True), lambda st: sweep([qi], st, True), state)
        n = qi - odd
        pair = (n // 2) % 2
        state = lax.cond(pair == 1, lambda st: sweep([n - 1, n - 2], st, False), lambda st: st, state)
        m = n - 2 * pair
        state = lax.fori_loop(
            0, m // 4, lambda i, st: sweep([m - 1 - 4 * i, m - 2 - 4 * i, m - 3 - 4 * i, m - 4 - 4 * i], st, False),
            state)
        for s in range(n_strips):
            o_ref[0, 0, rows(qi, s), :] = state[2 * s + 1]
        return unused

    lax.fori_loop(0, q_ref.shape[2] // blk, query_block, 0)


def _sb_prompt(q, k, v, bias):
    B, H, T, _ = q.shape
    spec = pl.BlockSpec((1, 1, T, HEAD_DIM), lambda b, h: (b, h, 0, 0))
    return pl.pallas_call(
        functools.partial(_sb_prompt_kernel, blk=SB_BLOCK, strip=SB_STRIP),
        grid=(B, H),
        in_specs=[pl.BlockSpec(memory_space=pltpu.SMEM), spec, spec, spec],
        out_specs=spec,
        out_shape=jax.ShapeDtypeStruct(q.shape, F32),
        compiler_params=_params(("arbitrary", "arbitrary")),
        name="sb_prompt",
    )(bias, q, k, v)


def _sb_sample_kernel(pt_ref, q_ref, kn_ref, vn_ref, bias_ref, *refs, pages_per_step):
    P = pages_per_step
    k_pages = refs[:P]
    v_pages = refs[P:2 * P]
    o_ref, carry_scr, acc_scr = refs[2 * P:]
    g = pl.program_id(1)
    T = q_ref.shape[1]
    R = SB_HEADS * T
    q = q_ref[0] * (HEAD_DIM ** -0.5)
    q_rows = jnp.concatenate([q] * SB_HEADS, axis=0)
    row_head = _idiv(lax.broadcasted_iota(jnp.int32, (R, SB_W), 0), T)
    lane_head = _idiv(lax.broadcasted_iota(jnp.int32, (R, SB_W), 1), HEAD_DIM)
    own = row_head == lane_head
    q_hi, q_lo = _split(jnp.where(own, q_rows, 0.0), 2)
    q_both = jnp.concatenate([q_hi, q_lo], axis=0)
    U = _suffix_matrix(PAGE_SIZE)
    bias = bias_ref[...]

    def run(blocks, carry, acc, transposed):
        qk = _dot if transposed else _dot_nt
        av = _dot_nt if transposed else _dot
        zs = []
        for k, _, _ in blocks:
            k_hi, k_lo = _split(k, 2)
            d = qk(q_both, k_hi)
            zs.append(d[:R] + d[R:] + qk(q_hi, k_lo) + bias)
        sps = [jnp.maximum(z, 0.0) + jnp.log(1.0 + jnp.exp(-jnp.abs(z))) for z in zs]
        lns = [-sp if m is None else jnp.where(m, -sp, 0.0) for sp, (_, _, m) in zip(sps, blocks)]
        boths = [jnp.dot(jnp.concatenate(_split(ln, 2), axis=0), U, preferred_element_type=F32) for ln in lns]
        locs = [b[:R] + b[R:] for b in boths]
        for z, sp, ln, loc, (_, v, m) in zip(zs, sps, lns, locs, blocks):
            a = jnp.exp(z - sp + (loc + carry))
            if m is not None:
                a = jnp.where(m, a, 0.0)
            carry = carry + (loc[:, 0:1] + ln[:, 0:1])
            a_hi, a_lo = _split(a, 2)
            v_hi, v_lo = _split(v, 2)
            d = av(jnp.concatenate([a_hi, a_lo], axis=0), v_hi)
            acc = acc + (d[:R] + d[R:] + av(a_hi, v_lo))
        return carry, acc

    @pl.when(g == 0)
    def _():
        pad = jnp.zeros((PAGE_SIZE - T, SB_W), F32)
        k_new = jnp.concatenate([kn_ref[0], pad], axis=0)
        v_new = jnp.concatenate([vn_ref[0], pad], axis=0)
        tok = _imod(lax.broadcasted_iota(jnp.int32, (R, PAGE_SIZE), 0), T)
        key = lax.broadcasted_iota(jnp.int32, (R, PAGE_SIZE), 1)
        carry, acc = run([(k_new, v_new, key < tok)], jnp.zeros((R, 1), F32), jnp.zeros((R, SB_W), F32), False)
        carry_scr[...] = carry
        acc_scr[...] = acc

    blocks = [(k_pages[j][0, 0], v_pages[j][0, 0], None) for j in reversed(range(P))]
    carry, acc = run(blocks, carry_scr[...], acc_scr[...], True)
    carry_scr[...] = carry
    acc_scr[...] = acc

    @pl.when(g == pl.num_programs(1) - 1)
    def _():
        mine = jnp.where(own, acc_scr[...], 0.0)
        out = mine[0:T]
        for h in range(1, SB_HEADS):
            out = out + mine[h * T:(h + 1) * T]
        o_ref[0] = out


def _sb_sample(q, k_new, v_new, cache_k, cache_v, page_table_flat, bias_rows, layer, n_pages):
    DB, T, _ = q.shape
    P = SB_PAGES_PER_STEP
    n_steps = n_pages // P
    tok = pl.BlockSpec((1, T, SB_W), lambda b, g, pt: (b, 0, 0))

    def page_spec(j):
        return pl.BlockSpec((1, 1, SB_W, PAGE_SIZE),
                            lambda b, g, pt, j=j: (layer, pt[b * n_pages + (n_steps - 1 - g) * P + j], 0, 0))

    grid_spec = pltpu.PrefetchScalarGridSpec(
        num_scalar_prefetch=1,
        grid=(DB, n_steps),
        in_specs=[tok, tok, tok, pl.BlockSpec((SB_HEADS * T, PAGE_SIZE), lambda b, g, pt: (0, 0))]
        + [page_spec(j) for j in range(P)] * 2,
        out_specs=tok,
        scratch_shapes=[pltpu.VMEM((SB_HEADS * T, 1), F32), pltpu.VMEM((SB_HEADS * T, SB_W), F32)],
    )
    return pl.pallas_call(
        functools.partial(_sb_sample_kernel, pages_per_step=P),
        grid_spec=grid_spec,
        out_shape=jax.ShapeDtypeStruct(q.shape, F32),
        compiler_params=_params(("arbitrary", "arbitrary")),
        name="sb_sample",
    )(page_table_flat, q, k_new, v_new, bias_rows, *([cache_k] * P), *([cache_v] * P))


def _route(logits):
    lane_i = lax.broadcasted_iota(jnp.int32, logits.shape, 1)
    lane = lane_i.astype(F32)
    neg = -jnp.inf
    none = float(ROUTER_LANES)
    is_g = lane_i < N_GROUPS
    lg = jnp.where(is_g, logits, neg)
    mg = jnp.max(lg, axis=-1, keepdims=True)
    pg_top = 1.0 / jnp.sum(jnp.where(is_g, jnp.exp(lg - mg), 0.0), axis=-1, keepdims=True)
    g_idx = jnp.min(jnp.where(lg == mg, lane, none), axis=-1, keepdims=True)
    e_rel = lane_i - EXPERT_LANE0
    lane_group = _idiv(e_rel, EXPERTS_PER_GROUP).astype(F32)
    in_group = jnp.logical_and(jnp.logical_and(e_rel >= 0, e_rel < N_EXPERTS), lane_group == g_idx)
    el = jnp.where(in_group, logits, neg)
    m1 = jnp.max(el, axis=-1, keepdims=True)
    i1 = jnp.min(jnp.where(el == m1, lane, none), axis=-1, keepdims=True)
    el2 = jnp.where(lane == i1, neg, el)
    m2 = jnp.max(el2, axis=-1, keepdims=True)
    i2 = jnp.min(jnp.where(el2 == m2, lane, none), axis=-1, keepdims=True)
    t = jnp.exp(m2 - m1)
    w1 = 1.0 / (1.0 + t)
    w2 = t / (1.0 + t)
    return jnp.where(lane == i1, pg_top * w1, 0.0) + jnp.where(lane == i2, pg_top * w2, 0.0)


def _outmoe_kernel(x_ref, yr_ref, yg_ref, ys_ref, g1_ref, sc2_ref, sh2_ref, g2_ref, n2g_ref, fg_ref, wout_ref,
                   wr_ref, br_ref, wg_ref, wu_ref, wd_ref, o_ref, x1_scr, h2_scr, dw_scr, acc_scr, *, final, hp):
    nb, tt, d = x_ref.shape
    rows = nb * tt
    e = pl.program_id(2)

    @pl.when(e == 0)
    def _():
        flat = lambda ref: ref[...].reshape(rows, ref.shape[-1])
        mixed = (_dot(flat(yr_ref), wout_ref[0:RWKV_W], hp)
                 + _dot(flat(yg_ref), wout_ref[RWKV_W:RWKV_W + GMLP_W], hp)
                 + _dot(flat(ys_ref), wout_ref[RWKV_W + GMLP_W:], hp))
        x1 = x_ref[...] + g1_ref[...] * mixed.reshape(nb, tt, d)
        h2 = _rms_mod(x1, n2g_ref[...], sc2_ref[...], sh2_ref[...]).reshape(rows, d)
        x1_scr[...] = x1
        h2_scr[...] = h2.astype(BF16)
        logits = _dot(h2, wr_ref[...], hp=True) + br_ref[...]
        dw_scr[...] = _route(logits)
        acc_scr[...] = jnp.zeros_like(acc_scr)

    hb = h2_scr[...]
    lane = lax.broadcasted_iota(jnp.int32, (rows, ROUTER_LANES), 1)
    dw = dw_scr[...]
    part = None
    for i in range(EXPERTS_PER_STEP):
        gate = jnp.dot(hb, wg_ref[0, i], preferred_element_type=F32)
        up = jnp.dot(hb, wu_ref[0, i], preferred_element_type=F32)
        w_e = jnp.sum(jnp.where(lane == e * EXPERTS_PER_STEP + (i + EXPERT_LANE0), dw, 0.0), axis=-1, keepdims=True)
        hid = gate * _sigmoid(gate) * up * w_e
        down = jnp.dot(hid.astype(BF16), wd_ref[0, i], preferred_element_type=F32)
        part = down if part is None else part + down
    acc_scr[...] += part

    @pl.when(e == pl.num_programs(2) - 1)
    def _():
        x2 = x1_scr[...] + g2_ref[...] * acc_scr[...].reshape(nb, tt, d)
        if final:
            x2 = x2 * lax.rsqrt(jnp.mean(x2 * x2, axis=-1, keepdims=True) + RMS_EPS) * fg_ref[...]
        o_ref[...] = x2


def _out_moe(x, y_rwkv, y_gmlp, y_sb, gate1, scale2, shift2, gate2, lw, layer, final_g, final, nb, tt, hp):
    B, T, _ = x.shape
    rows = nb * tt
    tok = lambda w: pl.BlockSpec((nb, tt, w), lambda i, j, e: (i, j, 0))
    per_seq = pl.BlockSpec((nb, 1, D_MODEL), lambda i, j, e: (i, 0, 0))
    vec = pl.BlockSpec((1, 1, D_MODEL), lambda i, j, e: (0, 0, 0))
    return pl.pallas_call(
        functools.partial(_outmoe_kernel, final=final, hp=hp),
        grid=(B // nb, T // tt, N_EXPERTS // EXPERTS_PER_STEP),
        in_specs=[tok(D_MODEL), tok(RWKV_W), tok(GMLP_W), tok(SB_W), per_seq, per_seq, per_seq, per_seq, vec, vec,
                  pl.BlockSpec((D_MODEL, D_MODEL), lambda i, j, e: (0, 0)),
                  pl.BlockSpec((D_MODEL, ROUTER_LANES), lambda i, j, e: (0, 0)),
                  pl.BlockSpec((1, ROUTER_LANES), lambda i, j, e: (0, 0)),
                  pl.BlockSpec((1, EXPERTS_PER_STEP, D_MODEL, D_EXPERT), lambda i, j, e: (layer, e, 0, 0)),
                  pl.BlockSpec((1, EXPERTS_PER_STEP, D_MODEL, D_EXPERT), lambda i, j, e: (layer, e, 0, 0)),
                  pl.BlockSpec((1, EXPERTS_PER_STEP, D_EXPERT, D_MODEL), lambda i, j, e: (layer, e, 0, 0))],
        out_specs=tok(D_MODEL),
        out_shape=jax.ShapeDtypeStruct(x.shape, F32),
        scratch_shapes=[pltpu.VMEM((nb, tt, D_MODEL), F32), pltpu.VMEM((rows, D_MODEL), BF16),
                        pltpu.VMEM((rows, ROUTER_LANES), F32), pltpu.VMEM((rows, D_MODEL), F32)],
        compiler_params=_params(("arbitrary", "arbitrary", "arbitrary")),
        name="out_moe",
    )(x, y_rwkv, y_gmlp, y_sb, gate1, scale2, shift2, gate2, lw['norm2_g'], final_g,
      lw['w_out_f32'] if hp else lw['w_out'], lw['w_router'], lw['b_router'], lw['exp_wg'], lw['exp_wu'], lw['exp_wd'])


def _trunk_layer(x, mod, lw, layer, s0_full, prev0, gm_ws, gm_bias, sb_fn, final_g, final, nb_tok_in, tt_in,
                 nb_tok_out, tt_out, rwkv_ns, rwkv_ts, hp):
    B, T, _ = x.shape
    shift1, scale1, gate1, shift2, scale2, gate2 = mod
    p_rwkv, g_u, g_v, q, k, v = _in_projection(x, scale1, shift1, lw['norm1_g'],
                                               lw['w_in_f32'] if hp else lw['w_in'], nb_tok_in, tt_in, hp)
    y_rwkv, s_new = _rwkv_mix(p_rwkv, prev0, s0_full, lw['rwkv'], rwkv_ns, rwkv_ts, hp)
    n_chunks = B * T // CHUNK
    y_gmlp, v_rows = _gmlp_mix(g_u.reshape(n_chunks, CHUNK, GMLP_W), g_v.reshape(n_chunks, CHUNK, GMLP_W),
                               lw['gmlp_ln_g'], lw['gmlp_ln_b'], gm_ws, gm_bias, hp)
    y_gmlp = y_gmlp.reshape(B, T, GMLP_W)
    v_rows = v_rows.reshape(B, T, GMLP_W)
    y_sb = sb_fn(q, k, v)
    x_out = _out_moe(x, y_rwkv, y_gmlp, y_sb, gate1, scale2, shift2, gate2, lw, layer, final_g, final,
                     nb_tok_out, tt_out, hp)
    return x_out, s_new, p_rwkv, k, v, v_rows


def _pad_rows(w, row0):
    return jnp.zeros((LORA_W, RWKV_W), F32).at[row0:row0 + w.shape[0]].set(w)


def kernel(x_prompt, x_sample, cache_k, cache_v, page_table, state_rwkv, state_rwkv_shift, c_prompt, c_sample, norm1_g, norm2_g, w_mod, b_mod, w_in, rwkv_mu, rwkv_w0, rwkv_w_up, rwkv_a0, rwkv_a_up, rwkv_g_up, rwkv_k_k, rwkv_k_a, rwkv_r_k, rwkv_ln_g, rwkv_ln_b, gmlp_ln_g, gmlp_ln_b, gmlp_ws, gmlp_bs, sb_bias, w_out, router_g_w, router_g_b, router_e_w, router_e_b, exp_wg, exp_wu, exp_wd, final_g):
    B, T, _ = x_prompt.shape
    DB, TS, _ = x_sample.shape
    depth = w_in.shape[0]
    n_pages = page_table.shape[1]
    n_pool = cache_k.shape[1]

    mod_all = _modulation(jnp.concatenate([c_prompt, c_sample], axis=0), w_mod, b_mod)
    cache_k4 = jnp.transpose(cache_k, (0, 1, 3, 4, 2)).reshape(depth, n_pool, SB_W, PAGE_SIZE)
    cache_v4 = jnp.transpose(cache_v, (0, 1, 3, 4, 2)).reshape(depth, n_pool, SB_W, PAGE_SIZE)
    pt_flat = page_table.reshape(-1)
    exp_wg_b = exp_wg.astype(BF16)
    exp_wu_b = exp_wu.astype(BF16)
    exp_wd_b = exp_wd.astype(BF16)
    fg = final_g.reshape(1, 1, D_MODEL)
    seqs_per_chunk = CHUNK // TS

    xp, xs = x_prompt, x_sample
    outs = {n: [] for n in ('kp', 'vp', 'ks', 'vs', 'Sp', 'shp', 'Ss', 'shs', 'gv')}
    for l in range(depth):
        w_router = jnp.zeros((D_MODEL, ROUTER_LANES), F32)
        w_router = w_router.at[:, :N_GROUPS].set(router_g_w[l]).at[:, EXPERT_LANE0:EXPERT_LANE0 + N_EXPERTS].set(router_e_w[l])
        b_router = jnp.zeros((1, ROUTER_LANES), F32)
        b_router = b_router.at[0, :N_GROUPS].set(router_g_b[l]).at[0, EXPERT_LANE0:EXPERT_LANE0 + N_EXPERTS].set(router_e_b[l])
        row = lambda a: a.reshape(1, -1)
        lw = dict(
            norm1_g=norm1_g[l], norm2_g=norm2_g[l].reshape(1, 1, D_MODEL), w_in=w_in[l].astype(BF16),
            w_in_f32=w_in[l], w_out=w_out[l].astype(BF16), w_out_f32=w_out[l],
            w_router=w_router, b_router=b_router,
            exp_wg=exp_wg_b, exp_wu=exp_wu_b, exp_wd=exp_wd_b,
            gmlp_ln_g=row(gmlp_ln_g[l]), gmlp_ln_b=row(gmlp_ln_b[l]),
            rwkv=dict(mu=row(rwkv_mu[l]), w0=row(rwkv_w0[l]), w_up=_pad_rows(rwkv_w_up[l], 0), a0=row(rwkv_a0[l]),
                      a_up=_pad_rows(rwkv_a_up[l], W_LORA), g_up=_pad_rows(rwkv_g_up[l], W_LORA + A_LORA),
                      k_k=row(rwkv_k_k[l]), k_a=row(rwkv_k_a[l]), r_k=row(rwkv_r_k[l]), ln_g=row(rwkv_ln_g[l]),
                      ln_b=row(rwkv_ln_b[l])),
        )
        final = l == depth - 1
        mods_p = [mod_all[l, i, :B].reshape(B, 1, D_MODEL) for i in range(N_MOD)]
        mods_s = [mod_all[l, i, B:].reshape(DB, 1, D_MODEL) for i in range(N_MOD)]

        bias_p = jnp.repeat(jnp.swapaxes(gmlp_bs[l], 0, 1), HEAD_DIM, axis=1)

        def sb_prompt(q, k, v, l=l):
            hm = lambda a: jnp.transpose(a.reshape(B, T, SB_HEADS, HEAD_DIM), (0, 2, 1, 3))
            o = _sb_prompt(hm(q), hm(k), hm(v), sb_bias[l])
            return jnp.transpose(o, (0, 2, 1, 3)).reshape(B, T, SB_W)

        xp, Sp, pp, kp, vp, _ = _trunk_layer(
            xp, mods_p, lw, l, jnp.zeros((B, RWKV_W, RWKV_W), F32), jnp.zeros((B, 1, RWKV_PROJ), F32),
            gmlp_ws[l], bias_p, sb_prompt, fg, final, 1, TOK_TILE_IN, 1, TOK_TILE_OUT, RWKV_SEQS_PER_STEP,
            RWKV_CHUNK, False)

        ws_s = jnp.einsum('ab,gts->gatbs', jnp.eye(seqs_per_chunk, dtype=F32), gmlp_ws[l][:, :TS, :TS])
        ws_s = ws_s.reshape(GMLP_GROUPS, CHUNK, CHUNK)
        bias_s = jnp.tile(bias_p[:TS], (seqs_per_chunk, 1))
        bias_rows = jnp.broadcast_to(jnp.repeat(sb_bias[l], TS)[:, None], (SB_HEADS * TS, PAGE_SIZE))

        def sb_sample(q, k, v, l=l, bias_rows=bias_rows):
            return _sb_sample(q, k, v, cache_k4, cache_v4, pt_flat, bias_rows, l, n_pages)

        xs, Ss, ps_, ksm, vsm, gv = _trunk_layer(
            xs, mods_s, lw, l, _state_to_full(state_rwkv[l]), state_rwkv_shift[l].reshape(DB, 1, RWKV_PROJ),
            ws_s, bias_s, sb_sample, fg, final, TOK_TILE_SAMPLE // TS, TS, TOK_TILE_OUT // TS, TS,
            RWKV_CHUNK // TS, TS, True)

        outs['kp'].append(kp.reshape(B, T, SB_HEADS, HEAD_DIM))
        outs['vp'].append(vp.reshape(B, T, SB_HEADS, HEAD_DIM))
        outs['ks'].append(ksm.reshape(DB, TS, SB_HEADS, HEAD_DIM))
        outs['vs'].append(vsm.reshape(DB, TS, SB_HEADS, HEAD_DIM))
        outs['Sp'].append(_state_from_full(Sp))
        outs['shp'].append(pp[:, -1])
        outs['Ss'].append(_state_from_full(Ss))
        outs['shs'].append(ps_[:, -1])
        outs['gv'].append(gv)

    st = lambda n: jnp.stack(outs[n])
    return (xp, xs, st('kp'), st('vp'), st('ks'), st('vs'), st('Sp'), st('shp'), st('Ss'), st('shs'), st('gv'))
```

```python
import functools

import jax
import jax.numpy as jnp
from jax import lax
from jax.experimental import pallas as pl
from jax.experimental.pallas import tpu as pltpu

F32 = jnp.float32
BF16 = jnp.bfloat16

D_MODEL = 1024
HEAD_DIM = 64
RWKV_HEADS = 4
RWKV_W = RWKV_HEADS * HEAD_DIM
LORA_W = 128
RWKV_PROJ = 3 * RWKV_W + LORA_W
W_LORA, A_LORA, G_LORA = 32, 32, 64
GN_EPS = 64e-5
GMLP_GROUPS = 4
GMLP_W = GMLP_GROUPS * HEAD_DIM
CHUNK = 128
LN_EPS = 1e-5
SB_HEADS = 8
SB_W = SB_HEADS * HEAD_DIM
PAGE_SIZE = 128
N_GROUPS = 4
EXPERTS_PER_GROUP = 4
N_EXPERTS = N_GROUPS * EXPERTS_PER_GROUP
D_EXPERT = 256
N_MOD = 6
RMS_EPS = 1e-6
SEG_WIDTHS = (RWKV_PROJ, GMLP_W, GMLP_W, SB_W, SB_W, SB_W)

LANES = 128
ROUTER_LANES = LANES
EXPERT_LANE0 = N_GROUPS
VMEM_LIMIT = 56 * 1024 * 1024

RWKV_CHUNK = 64
SB_BLOCK = 256
SB_STRIP = 64
RWKV_SEQS_PER_STEP = 4
EXPERTS_PER_STEP = 4
SB_PAGES_PER_STEP = 16
TOK_TILE_IN = 512
TOK_TILE_SAMPLE = 256
TOK_TILE_OUT = 512
GMLP_CHUNKS_PER_STEP = 4


def _params(sem):
    return pltpu.CompilerParams(dimension_semantics=sem, vmem_limit_bytes=VMEM_LIMIT)


def _mm(a, b, dims, hp):
    dg = lambda x, y: lax.dot_general(x, y, (dims, ((), ())), preferred_element_type=F32)
    if not hp:
        return dg(a.astype(BF16), b.astype(BF16))
    a_hi, a_lo = _split(a.astype(F32), 2)
    b_hi, b_lo = _split(b.astype(F32), 2)
    return dg(a_hi, b_hi) + dg(a_hi, b_lo) + dg(a_lo, b_hi)


def _dot(a, b, hp=False):
    return _mm(a, b, ((1,), (0,)), hp)


def _dot_nt(a, b, hp=False):
    return _mm(a, b, ((1,), (1,)), hp)


def _dot_tn(a, b, hp=False):
    return _mm(a, b, ((0,), (0,)), hp)


def _split(x, n):
    terms = []
    rem = x
    for _ in range(n):
        t = rem.astype(BF16)
        terms.append(t)
        rem = rem - t.astype(F32)
    return terms


def _dot_exact_rhs(x, m, n=2):
    out = None
    for t in _split(x, n):
        d = jnp.dot(t, m, preferred_element_type=F32)
        out = d if out is None else out + d
    return out


def _dot_exact_lhs(m, x, n=2):
    out = None
    for t in _split(x, n):
        d = jnp.dot(m, t, preferred_element_type=F32)
        out = d if out is None else out + d
    return out


def _softplus(z):
    return jnp.maximum(z, 0.0) + jnp.log1p(jnp.exp(-jnp.abs(z)))


def _sigmoid(z):
    return 1.0 / (1.0 + jnp.exp(-z))


def _gelu_tanh(x):
    return 0.5 * x * (1.0 + jnp.tanh(0.7978845608028654 * (x + 0.044715 * (x * x * x))))


def _idiv(x, n):
    assert n & (n - 1) == 0
    return jnp.right_shift(x, n.bit_length() - 1)


def _imod(x, n):
    assert n & (n - 1) == 0
    return jnp.bitwise_and(x, n - 1)


def _head_block_mask(n, dtype=BF16):
    r = _idiv(lax.broadcasted_iota(jnp.int32, (n, n), 0), HEAD_DIM)
    c = _idiv(lax.broadcasted_iota(jnp.int32, (n, n), 1), HEAD_DIM)
    return jnp.where(r == c, 1.0, 0.0).astype(dtype)


def _mod_kernel(c_ref, w_ref, b_ref, o_ref):
    c = c_ref[...]
    o_ref[0, 0] = _dot(c * _sigmoid(c), w_ref[0], hp=True) + b_ref[0, 0]


def _modulation(c_all, w_mod, b_mod):
    depth = w_mod.shape[0]
    rows = c_all.shape[0]
    return pl.pallas_call(
        _mod_kernel,
        grid=(depth, N_MOD),
        in_specs=[
            pl.BlockSpec((rows, D_MODEL), lambda l, j: (0, 0)),
            pl.BlockSpec((1, D_MODEL, D_MODEL), lambda l, j: (l, 0, j)),
            pl.BlockSpec((1, 1, 1, D_MODEL), lambda l, j: (l, j, 0, 0)),
        ],
        out_specs=pl.BlockSpec((1, 1, rows, D_MODEL), lambda l, j: (l, j, 0, 0)),
        out_shape=jax.ShapeDtypeStruct((depth, N_MOD, rows, D_MODEL), F32),
        compiler_params=_params(("arbitrary", "arbitrary")),
        name="modulation",
    )(c_all, w_mod, b_mod.reshape(depth, N_MOD, 1, D_MODEL))


def _rms_mod(x, g, scale, shift):
    y = x * lax.rsqrt(jnp.mean(x * x, axis=-1, keepdims=True) + RMS_EPS) * g
    return y * (1.0 + scale) + shift


def _inproj_kernel(x_ref, sc_ref, sh_ref, g_ref, w_ref, *o_refs, hp):
    nb, tt, d = x_ref.shape
    h = _rms_mod(x_ref[...], g_ref[...], sc_ref[...], sh_ref[...]).reshape(nb * tt, d)
    if not hp:
        h = h.astype(BF16)
    off = 0
    for o_ref in o_refs:
        w = o_ref.shape[-1]
        o_ref[...] = _dot(h, w_ref[:, off:off + w], hp).reshape(nb, tt, w)
        off += w


def _in_projection(x, scale, shift, g, w_in, nb, tt, hp):
    B, T, _ = x.shape
    p_total = w_in.shape[1]
    tok = lambda w: pl.BlockSpec((nb, tt, w), lambda i, j: (i, j, 0))
    per_seq = pl.BlockSpec((nb, 1, D_MODEL), lambda i, j: (i, 0, 0))
    return pl.pallas_call(
        functools.partial(_inproj_kernel, hp=hp),
        grid=(B // nb, T // tt),
        in_specs=[tok(D_MODEL), per_seq, per_seq,
                  pl.BlockSpec((1, 1, D_MODEL), lambda i, j: (0, 0, 0)),
                  pl.BlockSpec((D_MODEL, p_total), lambda i, j: (0, 0))],
        out_specs=[tok(w) for w in SEG_WIDTHS],
        out_shape=[jax.ShapeDtypeStruct((B, T, w), F32) for w in SEG_WIDTHS],
        compiler_params=_params(("arbitrary", "arbitrary")),
        name="in_projection",
    )(x, scale, shift, g.reshape(1, 1, D_MODEL), w_in)


def _rwkv_kernel(p_ref, prev0_ref, s0_ref, mu_ref, w0_ref, wup_ref, a0_ref, aup_ref, gup_ref, kk_ref, ka_ref,
                 rk_ref, lng_ref, lnb_ref, y_ref, s_out_ref, s_scr, carry_scr, *, ns, ts, hp):
    C = ns * ts
    W = RWKV_W
    chunk = pl.program_id(1)
    dot = functools.partial(_dot, hp=hp)
    dot_nt = functools.partial(_dot_nt, hp=hp)
    dot_tn = functools.partial(_dot_tn, hp=hp)

    @pl.when(chunk == 0)
    def _():
        s_scr[...] = s0_ref[...]
        carry_scr[...] = prev0_ref[...]

    p = p_ref[...].reshape(C, RWKV_PROJ)
    row1 = lax.broadcasted_iota(jnp.int32, (C, 1), 0)
    prev_first = jnp.broadcast_to(carry_scr[...], (ns, ts, RWKV_PROJ)).reshape(C, RWKV_PROJ)
    prev = jnp.where(_imod(row1, ts) == 0, prev_first, pltpu.roll(p, 1, 0))
    carry_scr[...] = p_ref[:, ts - 1:ts, :]
    ps = p + mu_ref[...] * (prev - p)

    r = ps[:, 0:W]
    k = ps[:, W:2 * W]
    v = ps[:, 2 * W:3 * W]
    lora = ps[:, 3 * W:]
    w = -_softplus(-(w0_ref[...] + dot(jnp.tanh(lora), wup_ref[...]))) - 0.5
    lw = -jnp.exp(w)
    a = _sigmoid(a0_ref[...] + dot(lora, aup_ref[...]))
    g = dot(_sigmoid(lora), gup_ref[...])

    head_bd = _head_block_mask(W)
    kkr = k * kk_ref[...]
    kk = kkr * lax.rsqrt(jnp.maximum(_dot_exact_rhs(kkr * kkr, head_bd), 1e-24))
    kh = k * (1.0 + (a - 1.0) * ka_ref[...])
    a_vec = -kk
    b_vec = kk * a

    rowi = lax.broadcasted_iota(jnp.int32, (C, C), 0)
    coli = lax.broadcasted_iota(jnp.int32, (C, C), 1)
    same = _idiv(rowi, ts) == _idiv(coli, ts)
    strict = jnp.logical_and(same, coli < rowi)
    incl = jnp.logical_and(same, coli <= rowi)
    L = _dot_exact_lhs(jnp.where(incl, 1.0, 0.0).astype(BF16), lw, 3)
    L_end = _dot_exact_lhs(jnp.where(same, 1.0, 0.0).astype(BF16), lw, 3)
    e_neg = jnp.exp(-L)
    e_end = jnp.exp(L_end - L)
    At = a_vec * jnp.exp(L - lw)
    Rt = r * jnp.exp(L)
    Bt = b_vec * e_neg
    Kt = kh * e_neg
    Bh = b_vec * e_end
    Kh = kh * e_end
    P_end = jnp.exp(L_end)

    lane_head = _idiv(lax.broadcasted_iota(jnp.int32, (1, W), 1), HEAD_DIM)
    eye =jnp.where(rowi == coli, 1.0, 0.0)
    n_factors = max(1, (ts - 1).bit_length())
    T_heads, rb_heads = [], []
    akv = jnp.zeros((C, W), F32)
    rkv = jnp.zeros((C, W), F32)
    for h in range(RWKV_HEADS):
        mh = lane_head == h
        At_h = jnp.where(mh, At, 0.0)
        Rt_h = jnp.where(mh, Rt, 0.0)
        ab = jnp.where(strict, dot_nt(At_h, Bt), 0.0)
        ak = jnp.where(strict, dot_nt(At_h, Kt), 0.0)
        rb = jnp.where(incl, dot_nt(Rt_h, Bt), 0.0)
        rk = jnp.where(incl, dot_nt(Rt_h, Kt), 0.0)
        T = eye + ab
        apow = ab
        for _ in range(n_factors - 1):
            apow = dot(apow, apow)
            T = T + dot(T, apow)
        akv = akv + jnp.where(mh, dot(ak, v), 0.0)
        rkv = rkv + jnp.where(mh, dot(rk, v), 0.0)
        T_heads.append(T)
        rb_heads.append(rb)

    xs, ys = [], []
    for q in range(ns):
        rows = slice(q * ts, (q + 1) * ts)
        Sq = s_scr[q]
        xs.append(dot_nt(At[rows], Sq))
        ys.append(dot_nt(Rt[rows], Sq))
    X = xs[0] if ns == 1 else jnp.concatenate(xs, axis=0)
    YS = ys[0] if ns == 1 else jnp.concatenate(ys, axis=0)
    rhs = X + akv
    U = jnp.zeros((C, W), F32)
    for h in range(RWKV_HEADS):
        U = U + jnp.where(lane_head == h, dot(T_heads[h], rhs), 0.0)
    Y = YS + rkv
    for h in range(RWKV_HEADS):
        Y = Y + jnp.where(lane_head == h, dot(rb_heads[h], U), 0.0)

    bd_f32 = _head_block_mask(W, F32)
    for q in range(ns):
        rows = slice(q * ts, (q + 1) * ts)
        upd = dot_tn(U[rows], Bh[rows]) + dot_tn(v[rows], Kh[rows])
        s_scr[q] = (s_scr[q] * P_end[q * ts:q * ts + 1, :] + upd) * bd_f32

    inv_n = 1.0 / HEAD_DIM
    mean = _dot_exact_rhs(Y, head_bd) * inv_n
    yc = Y - mean
    var = _dot_exact_rhs(yc * yc, head_bd) * inv_n
    yn = yc * lax.rsqrt(var + GN_EPS) * lng_ref[...] + lnb_ref[...]
    bonus = _dot_exact_rhs(r * kh * rk_ref[...], head_bd) * v
    y_ref[...] = ((yn + bonus) * g).reshape(ns, ts, W)

    @pl.when(chunk == pl.num_programs(1) - 1)
    def _():
        s_out_ref[...] = s_scr[...]


def _rwkv_mix(p_rwkv, prev0, s0_full, lw, ns, ts, hp):
    B, T, _ = p_rwkv.shape
    W = RWKV_W
    vec = lambda n: pl.BlockSpec((1, n), lambda i, j: (0, 0))
    mat = lambda: pl.BlockSpec((LORA_W, W), lambda i, j: (0, 0))
    kern = functools.partial(_rwkv_kernel, ns=ns, ts=ts, hp=hp)
    return pl.pallas_call(
        kern,
        grid=(B // ns, T // ts),
        in_specs=[
            pl.BlockSpec((ns, ts, RWKV_PROJ), lambda i, j: (i, j, 0)),
            pl.BlockSpec((ns, 1, RWKV_PROJ), lambda i, j: (i, 0, 0)),
            pl.BlockSpec((ns, W, W), lambda i, j: (i, 0, 0)),
            vec(RWKV_PROJ), vec(W), mat(), vec(W), mat(), mat(), vec(W), vec(W), vec(W), vec(W), vec(W),
        ],
        out_specs=[
            pl.BlockSpec((ns, ts, W), lambda i, j: (i, j, 0)),
            pl.BlockSpec((ns, W, W), lambda i, j: (i, 0, 0)),
        ],
        out_shape=[jax.ShapeDtypeStruct((B, T, W), F32), jax.ShapeDtypeStruct((B, W, W), F32)],
        scratch_shapes=[pltpu.VMEM((ns, W, W), F32), pltpu.VMEM((ns, 1, RWKV_PROJ), F32)],
        compiler_params=_params(("arbitrary", "arbitrary")),
        name="rwkv_mix",
    )(p_rwkv, prev0, s0_full, lw['mu'], lw['w0'], lw['w_up'], lw['a0'], lw['a_up'], lw['g_up'],
      lw['k_k'], lw['k_a'], lw['r_k'], lw['ln_g'], lw['ln_b'])


def _state_to_full(s):
    B = s.shape[0]
    eye = jnp.eye(RWKV_HEADS, dtype=s.dtype)
    return jnp.einsum('bhij,hg->bhigj', s, eye).reshape(B, RWKV_W, RWKV_W)


def _state_from_full(sf):
    B = sf.shape[0]
    s5 = sf.reshape(B, RWKV_HEADS, HEAD_DIM, RWKV_HEADS, HEAD_DIM)
    return jnp.stack([s5[:, h, :, h, :] for h in range(RWKV_HEADS)], axis=1)


def _gmlp_kernel(u_ref, v_ref, lng_ref, lnb_ref, ws_ref, bias_ref, y_ref, vrow_ref, *, hp):
    W = GMLP_W
    head_bd = _head_block_mask(W)
    inv_n = 1.0 / HEAD_DIM
    row = lax.broadcasted_iota(jnp.int32, (CHUNK, CHUNK), 0)
    col = lax.broadcasted_iota(jnp.int32, (CHUNK, CHUNK), 1)
    tril = col <= row
    lane_group = _idiv(lax.broadcasted_iota(jnp.int32, (1, W), 1), HEAD_DIM)
    wms = [jnp.where(tril, ws_ref[gi], 0.0) for gi in range(GMLP_GROUPS)]
    if not hp:
        wms = [wm.astype(BF16) for wm in wms]
    for ci in range(u_ref.shape[0]):
        u = _gelu_tanh(u_ref[ci])
        v = _gelu_tanh(v_ref[ci])
        mean = _dot_exact_rhs(v, head_bd) * inv_n
        vc = v - mean
        var = _dot_exact_rhs(vc * vc, head_bd) * inv_n
        vn = vc * lax.rsqrt(var + LN_EPS) * lng_ref[...] + lnb_ref[...]
        vrow_ref[ci] = vn
        mixed = bias_ref[...]
        for gi in range(GMLP_GROUPS):
            mixed = mixed + _dot(wms[gi], jnp.where(lane_group == gi, vn, 0.0), hp)
        y_ref[ci] = u * mixed


def _gmlp_mix(u, v, ln_g, ln_b, ws_chunk, bias_chunk, hp):
    n = u.shape[0]
    nc = GMLP_CHUNKS_PER_STEP
    blk = pl.BlockSpec((nc, CHUNK, GMLP_W), lambda i: (i, 0, 0))
    vec = pl.BlockSpec((1, GMLP_W), lambda i: (0, 0))
    return pl.pallas_call(
        functools.partial(_gmlp_kernel, hp=hp),
        grid=(n // nc,),
        in_specs=[blk, blk, vec, vec,
                  pl.BlockSpec((GMLP_GROUPS, CHUNK, CHUNK), lambda i: (0, 0, 0)),
                  pl.BlockSpec((CHUNK, GMLP_W), lambda i: (0, 0))],
        out_specs=[blk, blk],
        out_shape=[jax.ShapeDtypeStruct(u.shape, F32), jax.ShapeDtypeStruct(u.shape, F32)],
        compiler_params=_params(("arbitrary",)),
        name="gmlp_mix",
    )(u, v, ln_g, ln_b, ws_chunk, bias_chunk)


def _sb_block(qb, kb, vb, bias, U, carry, acc, mask, hp=False):
    m = qb.shape[0]
    z = _dot_nt(qb, kb, hp) + bias
    sp = jnp.maximum(z, 0.0) + jnp.log(1.0 + jnp.exp(-jnp.abs(z)))
    log_not = -sp
    if mask is not None:
        log_not = jnp.where(mask, log_not, 0.0)
    hi, lo = _split(log_not, 2)
    both = jnp.dot(jnp.concatenate([hi, lo], axis=0), U, preferred_element_type=F32)
    local = both[:m] + both[m:]
    a = jnp.exp(z - sp + (local + carry))
    if mask is not None:
        a = jnp.where(mask, a, 0.0)
    acc = acc + _dot(a, vb, hp)
    carry = carry + (local[:, 0:1] + log_not[:, 0:1])
    return carry, acc


def _suffix_matrix(n):
    r = lax.broadcasted_iota(jnp.int32, (n, n), 0)
    c = lax.broadcasted_iota(jnp.int32, (n, n), 1)
    return jnp.where(r > c, 1.0, 0.0).astype(BF16)


def _sb_prompt_kernel(bias_ref, q_ref, k_ref, v_ref, o_ref, *, blk, strip):
    pair = pl.program_id(1)
    biases = [bias_ref[2 * pair], bias_ref[2 * pair + 1]]
    n_strips = blk // strip
    U = _suffix_matrix(blk)
    row = lax.broadcasted_iota(jnp.int32, (strip, blk), 0)
    col = lax.broadcasted_iota(jnp.int32, (strip, blk), 1)
    first_head = lax.broadcasted_iota(jnp.int32, (1, 2 * HEAD_DIM), 1) < HEAD_DIM
    slots = [(hh, s) for hh in range(2) for s in range(n_strips)]

    def kv(j):
        start = pl.multiple_of(j * blk, blk)
        return k_ref[0, pl.ds(start, blk), :].astype(BF16), v_ref[0, pl.ds(start, blk), :].astype(BF16)

    def rows(qi, s):
        return pl.ds(pl.multiple_of(qi * blk + s * strip, strip), strip)

    def query_block(qi, unused):
        qs = {}
        for s in range(n_strips):
            q = q_ref[0, rows(qi, s), :] * (HEAD_DIM ** -0.5)
            qs[0, s] = jnp.where(first_head, q, 0.0).astype(BF16)
            qs[1, s] = jnp.where(first_head, 0.0, q).astype(BF16)

        def sweep(js, state, diag_first):
            units = [(t, c) for t in range(len(js)) for c in range(len(slots))]
            kvs = [kv(j) for j in js]
            masks = [(col < row + slots[c][1] * strip) if (diag_first and t == 0) else None for t, c in units]
            zs = [_dot_nt(qs[slots[c]], kvs[t][0]) + biases[slots[c][0]] for t, c in units]
            sps = [jnp.maximum(z, 0.0) + jnp.log(1.0 + jnp.exp(-jnp.abs(z))) for z in zs]
            lns = [-sp if m is None else jnp.where(m, -sp, 0.0) for sp, m in zip(sps, masks)]
            boths = [jnp.dot(jnp.concatenate(_split(ln, 2), axis=0), U, preferred_element_type=F32) for ln in lns]
            locs = [b[:strip] + b[strip:] for b in boths]
            state = list(state)
            for u, (t, c) in enumerate(units):
                a = jnp.exp(zs[u] - sps[u] + (locs[u] + state[2 * c]))
                if masks[u] is not None:
                    a = jnp.where(masks[u], a, 0.0)
                state[2 * c] = state[2 * c] + (locs[u][:, 0:1] + lns[u][:, 0:1])
                state[2 * c + 1] = state[2 * c + 1] + _dot(a, kvs[t][1])
            return tuple(state)

        state = tuple(jnp.zeros((strip, w), F32) for _ in slots for w in (1, 2 * HEAD_DIM))
        odd = qi % 2
        state = lax.cond(odd == 1, lambda st: sweep([qi, qi - 1], st, True), lambda st: sweep([qi], st, True), state)
        n = qi - odd
        two = (n // 2) % 2
        state = lax.cond(two == 1, lambda st: sweep([n - 1, n - 2], st, False), lambda st: st, state)
        m = n - 2 * two
        state = lax.fori_loop(
            0, m // 4, lambda i, st: sweep([m - 1 - 4 * i, m - 2 - 4 * i, m - 3 - 4 * i, m - 4 - 4 * i], st, False),
            state)
        for s in range(n_strips):
            o_ref[0, rows(qi, s), :] = jnp.where(first_head, state[2 * s + 1], state[2 * (n_strips + s) + 1])
        return unused

    lax.fori_loop(0, q_ref.shape[1] // blk, query_block, 0)


def _sb_prompt(q, k, v, bias):
    B, T, W = q.shape
    spec = pl.BlockSpec((1, T, 2 * HEAD_DIM), lambda b, p: (b, 0, p))
    return pl.pallas_call(
        functools.partial(_sb_prompt_kernel, blk=SB_BLOCK, strip=SB_STRIP),
        grid=(B, W // (2 * HEAD_DIM)),
        in_specs=[pl.BlockSpec(memory_space=pltpu.SMEM), spec, spec, spec],
        out_specs=spec,
        out_shape=jax.ShapeDtypeStruct(q.shape, F32),
        compiler_params=_params(("arbitrary", "arbitrary")),
        name="sb_prompt",
    )(bias, q, k, v)


def _sb_sample_kernel(pt_ref, q_ref, kn_ref, vn_ref, bias_ref, *refs, pages_per_step):
    P = pages_per_step
    k_pages = refs[:P]
    v_pages = refs[P:2 * P]
    o_ref, carry_scr, acc_scr = refs[2 * P:]
    g = pl.program_id(1)
    T = q_ref.shape[1]
    R = SB_HEADS * T
    q = q_ref[0] * (HEAD_DIM ** -0.5)
    q_rows = jnp.concatenate([q] * SB_HEADS, axis=0)
    row_head = _idiv(lax.broadcasted_iota(jnp.int32, (R, SB_W), 0), T)
    lane_head = _idiv(lax.broadcasted_iota(jnp.int32, (R, SB_W), 1), HEAD_DIM)
    own = row_head == lane_head
    q_hi, q_lo = _split(jnp.where(own, q_rows, 0.0), 2)
    q_both = jnp.concatenate([q_hi, q_lo], axis=0)
    U = _suffix_matrix(PAGE_SIZE)
    bias = bias_ref[...]

    def run(blocks, carry, acc, transposed):
        qk = _dot if transposed else _dot_nt
        av = _dot_nt if transposed else _dot
        zs = []
        for k, _, _ in blocks:
            k_hi, k_lo = _split(k, 2)
            d = qk(q_both, k_hi)
            zs.append(d[:R] + d[R:] + qk(q_hi, k_lo) + bias)
        sps = [jnp.maximum(z, 0.0) + jnp.log(1.0 + jnp.exp(-jnp.abs(z))) for z in zs]
        lns = [-sp if m is None else jnp.where(m, -sp, 0.0) for sp, (_, _, m) in zip(sps, blocks)]
        boths = [jnp.dot(jnp.concatenate(_split(ln, 2), axis=0), U, preferred_element_type=F32) for ln in lns]
        locs = [b[:R] + b[R:] for b in boths]
        for z, sp, ln, loc, (_, v, m) in zip(zs, sps, lns, locs, blocks):
            a = jnp.exp(z - sp + (loc + carry))
            if m is not None:
                a = jnp.where(m, a, 0.0)
            carry = carry + (loc[:, 0:1] + ln[:, 0:1])
            a_hi, a_lo = _split(a, 2)
            v_hi, v_lo = _split(v, 2)
            d = av(jnp.concatenate([a_hi, a_lo], axis=0), v_hi)
            acc = acc + (d[:R] + d[R:] + av(a_hi, v_lo))
        return carry, acc

    @pl.when(g == 0)
    def _():
        pad = jnp.zeros((PAGE_SIZE - T, SB_W), F32)
        k_new = jnp.concatenate([kn_ref[0], pad], axis=0)
        v_new = jnp.concatenate([vn_ref[0], pad], axis=0)
        tok = _imod(lax.broadcasted_iota(jnp.int32, (R, PAGE_SIZE), 0), T)
        key = lax.broadcasted_iota(jnp.int32, (R, PAGE_SIZE), 1)
        carry, acc = run([(k_new, v_new, key < tok)], jnp.zeros((R, 1), F32), jnp.zeros((R, SB_W), F32), False)
        carry_scr[...] = carry
        acc_scr[...] = acc

    blocks = [(k_pages[j][0, 0], v_pages[j][0, 0], None) for j in reversed(range(P))]
    carry, acc = run(blocks, carry_scr[...], acc_scr[...], True)
    carry_scr[...] = carry
    acc_scr[...] = acc

    @pl.when(g == pl.num_programs(1) - 1)
    def _():
        mine = jnp.where(own, acc_scr[...], 0.0)
        out = mine[0:T]
        for h in range(1, SB_HEADS):
            out = out + mine[h * T:(h + 1) * T]
        o_ref[0] = out


def _sb_sample(q, k_new, v_new, cache_k, cache_v, page_table_flat, bias_rows, layer, n_pages):
    DB, T, _ = q.shape
    P = SB_PAGES_PER_STEP
    n_steps = n_pages // P
    tok = pl.BlockSpec((1, T, SB_W), lambda b, g, pt: (b, 0, 0))

    def page_spec(j):
        return pl.BlockSpec((1, 1, SB_W, PAGE_SIZE),
                            lambda b, g, pt, j=j: (layer, pt[b * n_pages + (n_steps - 1 - g) * P + j], 0, 0))

    grid_spec = pltpu.PrefetchScalarGridSpec(
        num_scalar_prefetch=1,
        grid=(DB, n_steps),
        in_specs=[tok, tok, tok, pl.BlockSpec((SB_HEADS * T, PAGE_SIZE), lambda b, g, pt: (0, 0))]
        + [page_spec(j) for j in range(P)] * 2,
        out_specs=tok,
        scratch_shapes=[pltpu.VMEM((SB_HEADS * T, 1), F32), pltpu.VMEM((SB_HEADS * T, SB_W), F32)],
    )
    return pl.pallas_call(
        functools.partial(_sb_sample_kernel, pages_per_step=P),
        grid_spec=grid_spec,
        out_shape=jax.ShapeDtypeStruct(q.shape, F32),
        compiler_params=_params(("arbitrary", "arbitrary")),
        name="sb_sample",
    )(page_table_flat, q, k_new, v_new, bias_rows, *([cache_k] * P), *([cache_v] * P))


def _route(logits):
    lane_i = lax.broadcasted_iota(jnp.int32, logits.shape, 1)
    lane = lane_i.astype(F32)
    neg = -jnp.inf
    none = float(ROUTER_LANES)
    is_g = lane_i < N_GROUPS
    lg = jnp.where(is_g, logits, neg)
    mg = jnp.max(lg, axis=-1, keepdims=True)
    pg_top = 1.0 / jnp.sum(jnp.where(is_g, jnp.exp(lg - mg), 0.0), axis=-1, keepdims=True)
    g_idx = jnp.min(jnp.where(lg == mg, lane, none), axis=-1, keepdims=True)
    e_rel = lane_i - EXPERT_LANE0
    lane_group = _idiv(e_rel, EXPERTS_PER_GROUP).astype(F32)
    in_group = jnp.logical_and(jnp.logical_and(e_rel >= 0, e_rel < N_EXPERTS), lane_group == g_idx)
    el = jnp.where(in_group, logits, neg)
    m1 = jnp.max(el, axis=-1, keepdims=True)
    i1 = jnp.min(jnp.where(el == m1, lane, none), axis=-1, keepdims=True)
    el2 = jnp.where(lane == i1, neg, el)
    m2 = jnp.max(el2, axis=-1, keepdims=True)
    i2 = jnp.min(jnp.where(el2 == m2, lane, none), axis=-1, keepdims=True)
    t = jnp.exp(m2 - m1)
    w1 = 1.0 / (1.0 + t)
    w2 = t / (1.0 + t)
    return jnp.where(lane == i1, pg_top * w1, 0.0) + jnp.where(lane == i2, pg_top * w2, 0.0)


def _outmoe_kernel(x_ref, yr_ref, yg_ref, ys_ref, g1_ref, sc2_ref, sh2_ref, g2_ref, n2g_ref, fg_ref, wout_ref,
                   wr_ref, br_ref, wg_ref, wu_ref, wd_ref, o_ref, x1_scr, h2_scr, dw_scr, acc_scr, *, final, hp):
    nb, tt, d = x_ref.shape
    rows = nb * tt
    e = pl.program_id(2)

    @pl.when(e == 0)
    def _():
        flat = lambda ref: ref[...].reshape(rows, ref.shape[-1])
        mixed = (_dot(flat(yr_ref), wout_ref[0:RWKV_W], hp)
                 + _dot(flat(yg_ref), wout_ref[RWKV_W:RWKV_W + GMLP_W], hp)
                 + _dot(flat(ys_ref), wout_ref[RWKV_W + GMLP_W:], hp))
        x1 = x_ref[...] + g1_ref[...] * mixed.reshape(nb, tt, d)
        h2 = _rms_mod(x1, n2g_ref[...], sc2_ref[...], sh2_ref[...]).reshape(rows, d)
        x1_scr[...] = x1
        h2_scr[...] = h2.astype(BF16)
        logits = _dot(h2, wr_ref[...], hp=True) + br_ref[...]
        dw_scr[...] = _route(logits)
        acc_scr[...] = jnp.zeros_like(acc_scr)

    hb = h2_scr[...]
    lane = lax.broadcasted_iota(jnp.int32, (rows, ROUTER_LANES), 1)
    dw = dw_scr[...]
    part = None
    for i in range(EXPERTS_PER_STEP):
        gate = jnp.dot(hb, wg_ref[0, i], preferred_element_type=F32)
        up = jnp.dot(hb, wu_ref[0, i], preferred_element_type=F32)
        w_e = jnp.sum(jnp.where(lane == e * EXPERTS_PER_STEP + (i + EXPERT_LANE0), dw, 0.0), axis=-1, keepdims=True)
        hid = gate * _sigmoid(gate) * up * w_e
        down = jnp.dot(hid.astype(BF16), wd_ref[0, i], preferred_element_type=F32)
        part = down if part is None else part + down
    acc_scr[...] += part

    @pl.when(e == pl.num_programs(2) - 1)
    def _():
        x2 = x1_scr[...] + g2_ref[...] * acc_scr[...].reshape(nb, tt, d)
        if final:
            x2 = x2 * lax.rsqrt(jnp.mean(x2 * x2, axis=-1, keepdims=True) + RMS_EPS) * fg_ref[...]
        o_ref[...] = x2


def _out_moe(x, y_rwkv, y_gmlp, y_sb, gate1, scale2, shift2, gate2, lw, layer, final_g, final, nb, tt, hp):
    B, T, _ = x.shape
    rows = nb * tt
    tok = lambda w: pl.BlockSpec((nb, tt, w), lambda i, j, e: (i, j, 0))
    per_seq = pl.BlockSpec((nb, 1, D_MODEL), lambda i, j, e: (i, 0, 0))
    vec = pl.BlockSpec((1, 1, D_MODEL), lambda i, j, e: (0, 0, 0))
    return pl.pallas_call(
        functools.partial(_outmoe_kernel, final=final, hp=hp),
        grid=(B // nb, T // tt, N_EXPERTS // EXPERTS_PER_STEP),
        in_specs=[tok(D_MODEL), tok(RWKV_W), tok(GMLP_W), tok(SB_W), per_seq, per_seq, per_seq, per_seq, vec, vec,
                  pl.BlockSpec((D_MODEL, D_MODEL), lambda i, j, e: (0, 0)),
                  pl.BlockSpec((D_MODEL, ROUTER_LANES), lambda i, j, e: (0, 0)),
                  pl.BlockSpec((1, ROUTER_LANES), lambda i, j, e: (0, 0)),
                  pl.BlockSpec((1, EXPERTS_PER_STEP, D_MODEL, D_EXPERT), lambda i, j, e: (layer, e, 0, 0)),
                  pl.BlockSpec((1, EXPERTS_PER_STEP, D_MODEL, D_EXPERT), lambda i, j, e: (layer, e, 0, 0)),
                  pl.BlockSpec((1, EXPERTS_PER_STEP, D_EXPERT, D_MODEL), lambda i, j, e: (layer, e, 0, 0))],
        out_specs=tok(D_MODEL),
        out_shape=jax.ShapeDtypeStruct(x.shape, F32),
        scratch_shapes=[pltpu.VMEM((nb, tt, D_MODEL), F32), pltpu.VMEM((rows, D_MODEL), BF16),
                        pltpu.VMEM((rows, ROUTER_LANES), F32), pltpu.VMEM((rows, D_MODEL), F32)],
        compiler_params=_params(("arbitrary", "arbitrary", "arbitrary")),
        name="out_moe",
    )(x, y_rwkv, y_gmlp, y_sb, gate1, scale2, shift2, gate2, lw['norm2_g'], final_g,
      lw['w_out_f32'] if hp else lw['w_out'], lw['w_router'], lw['b_router'], lw['exp_wg'], lw['exp_wu'], lw['exp_wd'])


def _trunk_layer(x, mod, lw, layer, s0_full, prev0, gm_ws, gm_bias, sb_fn, final_g, final, nb_tok_in, tt_in,
                 nb_tok_out, tt_out, rwkv_ns, rwkv_ts, hp):
    B, T, _ = x.shape
    shift1, scale1, gate1, shift2, scale2, gate2 = mod
    p_rwkv, g_u, g_v, q, k, v = _in_projection(x, scale1, shift1, lw['norm1_g'],
                                               lw['w_in_f32'] if hp else lw['w_in'], nb_tok_in, tt_in, hp)
    y_rwkv, s_new = _rwkv_mix(p_rwkv, prev0, s0_full, lw['rwkv'], rwkv_ns, rwkv_ts, hp)
    n_chunks = B * T // CHUNK
    y_gmlp, v_rows = _gmlp_mix(g_u.reshape(n_chunks, CHUNK, GMLP_W), g_v.reshape(n_chunks, CHUNK, GMLP_W),
                               lw['gmlp_ln_g'], lw['gmlp_ln_b'], gm_ws, gm_bias, hp)
    y_gmlp = y_gmlp.reshape(B, T, GMLP_W)
    v_rows = v_rows.reshape(B, T, GMLP_W)
    y_sb = sb_fn(q, k, v)
    x_out = _out_moe(x, y_rwkv, y_gmlp, y_sb, gate1, scale2, shift2, gate2, lw, layer, final_g, final,
                     nb_tok_out, tt_out, hp)
    return x_out, s_new, p_rwkv, k, v, v_rows


def _pad_rows(w, row0):
    return jnp.zeros((LORA_W, RWKV_W), F32).at[row0:row0 + w.shape[0]].set(w)


def kernel(x_prompt, x_sample, cache_k, cache_v, page_table, state_rwkv, state_rwkv_shift, c_prompt, c_sample, norm1_g, norm2_g, w_mod, b_mod, w_in, rwkv_mu, rwkv_w0, rwkv_w_up, rwkv_a0, rwkv_a_up, rwkv_g_up, rwkv_k_k, rwkv_k_a, rwkv_r_k, rwkv_ln_g, rwkv_ln_b, gmlp_ln_g, gmlp_ln_b, gmlp_ws, gmlp_bs, sb_bias, w_out, router_g_w, router_g_b, router_e_w, router_e_b, exp_wg, exp_wu, exp_wd, final_g):
    B, T, _ = x_prompt.shape
    DB, TS, _ = x_sample.shape
    depth = w_in.shape[0]
    n_pages = page_table.shape[1]
    n_pool = cache_k.shape[1]

    mod_all = _modulation(jnp.concatenate([c_prompt, c_sample], axis=0), w_mod, b_mod)
    cache_k4 = jnp.transpose(cache_k, (0, 1, 3, 4, 2)).reshape(depth, n_pool, SB_W, PAGE_SIZE)
    cache_v4 = jnp.transpose(cache_v, (0, 1, 3, 4, 2)).reshape(depth, n_pool, SB_W, PAGE_SIZE)
    pt_flat = page_table.reshape(-1)
    exp_wg_b = exp_wg.astype(BF16)
    exp_wu_b = exp_wu.astype(BF16)
    exp_wd_b = exp_wd.astype(BF16)
    fg = final_g.reshape(1, 1, D_MODEL)
    seqs_per_chunk = CHUNK // TS

    xp, xs = x_prompt, x_sample
    outs = {n: [] for n in ('kp', 'vp', 'ks', 'vs', 'Sp', 'shp', 'Ss', 'shs', 'gv')}
    for l in range(depth):
        w_router = jnp.zeros((D_MODEL, ROUTER_LANES), F32)
        w_router = w_router.at[:, :N_GROUPS].set(router_g_w[l]).at[:, EXPERT_LANE0:EXPERT_LANE0 + N_EXPERTS].set(router_e_w[l])
        b_router = jnp.zeros((1, ROUTER_LANES), F32)
        b_router = b_router.at[0, :N_GROUPS].set(router_g_b[l]).at[0, EXPERT_LANE0:EXPERT_LANE0 + N_EXPERTS].set(router_e_b[l])
        row = lambda a: a.reshape(1, -1)
        lw = dict(
            norm1_g=norm1_g[l], norm2_g=norm2_g[l].reshape(1, 1, D_MODEL), w_in=w_in[l].astype(BF16),
            w_in_f32=w_in[l], w_out=w_out[l].astype(BF16), w_out_f32=w_out[l],
            w_router=w_router, b_router=b_router,
            exp_wg=exp_wg_b, exp_wu=exp_wu_b, exp_wd=exp_wd_b,
            gmlp_ln_g=row(gmlp_ln_g[l]), gmlp_ln_b=row(gmlp_ln_b[l]),
            rwkv=dict(mu=row(rwkv_mu[l]), w0=row(rwkv_w0[l]), w_up=_pad_rows(rwkv_w_up[l], 0), a0=row(rwkv_a0[l]),
                      a_up=_pad_rows(rwkv_a_up[l], W_LORA), g_up=_pad_rows(rwkv_g_up[l], W_LORA + A_LORA),
                      k_k=row(rwkv_k_k[l]), k_a=row(rwkv_k_a[l]), r_k=row(rwkv_r_k[l]), ln_g=row(rwkv_ln_g[l]),
                      ln_b=row(rwkv_ln_b[l])),
        )
        final = l == depth - 1
        mods_p = [mod_all[l, i, :B].reshape(B, 1, D_MODEL) for i in range(N_MOD)]
        mods_s = [mod_all[l, i, B:].reshape(DB, 1, D_MODEL) for i in range(N_MOD)]

        bias_p = jnp.repeat(jnp.swapaxes(gmlp_bs[l], 0, 1), HEAD_DIM, axis=1)

        def sb_prompt(q, k, v, l=l):
            return _sb_prompt(q, k, v, sb_bias[l])

        xp, Sp, pp, kp, vp, _ = _trunk_layer(
            xp, mods_p, lw, l, jnp.zeros((B, RWKV_W, RWKV_W), F32), jnp.zeros((B, 1, RWKV_PROJ), F32),
            gmlp_ws[l], bias_p, sb_prompt, fg, final, 1, TOK_TILE_IN, 1, TOK_TILE_OUT, RWKV_SEQS_PER_STEP,
            RWKV_CHUNK, False)

        ws_s = jnp.einsum('ab,gts->gatbs', jnp.eye(seqs_per_chunk, dtype=F32), gmlp_ws[l][:, :TS, :TS])
        ws_s = ws_s.reshape(GMLP_GROUPS, CHUNK, CHUNK)
        bias_s = jnp.tile(bias_p[:TS], (seqs_per_chunk, 1))
        bias_rows = jnp.broadcast_to(jnp.repeat(sb_bias[l], TS)[:, None], (SB_HEADS * TS, PAGE_SIZE))

        def sb_sample(q, k, v, l=l, bias_rows=bias_rows):
            return _sb_sample(q, k, v, cache_k4, cache_v4, pt_flat, bias_rows, l, n_pages)

        xs, Ss, ps_, ksm, vsm, gv = _trunk_layer(
            xs, mods_s, lw, l, _state_to_full(state_rwkv[l]), state_rwkv_shift[l].reshape(DB, 1, RWKV_PROJ),
            ws_s, bias_s, sb_sample, fg, final, TOK_TILE_SAMPLE // TS, TS, TOK_TILE_OUT // TS, TS,
            RWKV_CHUNK // TS, TS, True)

        outs['kp'].append(kp.reshape(B, T, SB_HEADS, HEAD_DIM))
        outs['vp'].append(vp.reshape(B, T, SB_HEADS, HEAD_DIM))
        outs['ks'].append(ksm.reshape(DB, TS, SB_HEADS, HEAD_DIM))
        outs['vs'].append(vsm.reshape(DB, TS, SB_HEADS, HEAD_DIM))
        outs['Sp'].append(_state_from_full(Sp))
        outs['shp'].append(pp[:, -1])
        outs['Ss'].append(_state_from_full(Ss))
        outs['shs'].append(ps_[:, -1])
        outs['gv'].append(gv)

    st = lambda n: jnp.stack(outs[n])
    return (xp, xs, st('kp'), st('vp'), st('ks'), st('vs'), st('Sp'), st('shp'), st('Ss'), st('shs'), st('gv'))
```
